```python
import math
import jax, jax.numpy as jnp
from jax import lax
import numpy as np

D_MODEL = 2048
BATCH = 1
SEQ = 16384
DEPTH = 1
DEC_BATCH = 128
DEC_SEQ = 1
PAST_LEN = 16384
PAGE_SIZE = 128

HEAD_DIM = 64
N_HEADS = 16
N_KV_HEADS = 4
GROUP = N_HEADS // N_KV_HEADS
ATTN_WIDTH = N_HEADS * HEAD_DIM
KV_WIDTH = N_KV_HEADS * HEAD_DIM
MIX_WIDTH = D_MODEL
CONV_WIDTH = MIX_WIDTH - ATTN_WIDTH
CONV_K = 3
WINDOW = 128
ATTN_SCALE = HEAD_DIM ** -0.5
ROPE_THETA = 500000.0
ROPE_DIM = HEAD_DIM // 4
IN_COLS = ATTN_WIDTH + 2 * KV_WIDTH + 3 * CONV_WIDTH
SPLITS = (ATTN_WIDTH,
          ATTN_WIDTH + KV_WIDTH,
          ATTN_WIDTH + 2 * KV_WIDTH,
          ATTN_WIDTH + 2 * KV_WIDTH + CONV_WIDTH,
          ATTN_WIDTH + 2 * KV_WIDTH + 2 * CONV_WIDTH)
PEER_HEADS = 8
N_KEYS = 128
N_EXPERTS = N_KEYS * N_KEYS
PEER_TOPK = 16
PEER_QDIM = 256
PEER_HALF = PEER_QDIM // 2
PEER_BLOCK = 128
PLE_DIM = 256
EPS = 1e-6
NEG = -1e30

kernel_name = 'swa_sink_shortconv_peer_hybrid_step'


def rms_norm(x, g):
    xf = x.astype(jnp.float32)
    y = xf * lax.rsqrt(jnp.mean(xf * xf, axis=-1, keepdims=True) + EPS)
    return (y * g.astype(jnp.float32)).astype(x.dtype)


def partial_rope(x, pos):
    half = ROPE_DIM // 2
    inv = ROPE_THETA ** (-jnp.arange(half, dtype=jnp.float32) / half)
    ang = pos.astype(jnp.float32)[:, None] * inv
    cos = jnp.cos(ang)[:, None, :]
    sin = jnp.sin(ang)[:, None, :]
    xr = x[..., :ROPE_DIM].astype(jnp.float32)
    x1, x2 = xr[..., :half], xr[..., half:]
    rot = jnp.concatenate([x1 * cos - x2 * sin, x2 * cos + x1 * sin], axis=-1).astype(x.dtype)
    return jnp.concatenate([rot, x[..., ROPE_DIM:]], axis=-1)


def sink_softmax(s, mask, sinks):
    sk = sinks.astype(jnp.float32).reshape(N_KV_HEADS, GROUP)[:, :, None, None]
    s = jnp.where(mask, s, NEG)
    m = jnp.maximum(jnp.max(s, axis=-1, keepdims=True), sk)
    e = jnp.where(mask, jnp.exp(s - m), 0.0)
    return e / (jnp.sum(e, axis=-1, keepdims=True) + jnp.exp(sk - m))


def swa_prompt(q, k, v, sinks):
    Bn, S = q.shape[:2]
    nb = S // WINDOW
    qb = q.reshape(Bn, nb, WINDOW, N_KV_HEADS, GROUP, HEAD_DIM)
    def band(t):
        tp = jnp.concatenate([jnp.zeros_like(t[:, :WINDOW]), t], axis=1)
        tp = tp.reshape(Bn, nb + 1, WINDOW, N_KV_HEADS, HEAD_DIM)
        return jnp.concatenate([tp[:, :-1], tp[:, 1:]], axis=2)
    kb, vb = band(k), band(v)
    qi = jnp.arange(WINDOW)[:, None] + WINDOW
    kj = jnp.arange(2 * WINDOW)[None, :]
    diff = qi - kj
    kpos = jnp.arange(nb)[:, None] * WINDOW - WINDOW + kj
    mask = ((diff >= 0) & (diff < WINDOW))[None] & (kpos >= 0)[:, None, :]
    s = jnp.einsum('bnqhgd,bnkhd->bnhgqk', qb, kb).astype(jnp.float32) * ATTN_SCALE
    p = sink_softmax(s, mask[None, :, None, None], sinks)
    o = jnp.einsum('bnhgqk,bnkhd->bnqhgd', p.astype(v.dtype), vb)
    return o.reshape(Bn, S, ATTN_WIDTH)


def swa_sample(q, k, v, k_prev, v_prev, sinks, pos):
    Bn, T = q.shape[:2]
    qb = q.reshape(Bn, T, N_KV_HEADS, GROUP, HEAD_DIM)
    k_all = jnp.concatenate([k_prev, k], axis=1)
    v_all = jnp.concatenate([v_prev, v], axis=1)
    kpos = pos[0] - WINDOW + jnp.arange(WINDOW + T)
    diff = pos[:, None] - kpos[None, :]
    mask = (diff >= 0) & (diff < WINDOW)
    s = jnp.einsum('bqhgd,bkhd->bhgqk', qb, k_all).astype(jnp.float32) * ATTN_SCALE
    p = sink_softmax(s, mask[None, None, None], sinks)
    o = jnp.einsum('bhgqk,bkhd->bqhgd', p.astype(v.dtype), v_all)
    return o.reshape(Bn, T, ATTN_WIDTH), k_all, v_all


def short_conv(u, prev, w):
    T = u.shape[1]
    ext = jnp.concatenate([prev, u], axis=1)
    y = ext[:, 0:T] * w[0]
    for j in range(1, CONV_K):
        y = y + ext[:, j:j + T] * w[j]
    return y, ext[:, T:]


def hybrid_mixer(a, pos, past, w_in, conv_w, sinks, g_attn_out, g_conv_out, w_out):
    Bn, T = a.shape[:2]
    z = jnp.einsum('bsd,dc->bsc', a, w_in)
    q, k, v, gate_b, gate_c, xc = jnp.split(z, SPLITS, axis=-1)
    q = partial_rope(q.reshape(Bn, T, N_HEADS, HEAD_DIM), pos)
    k = partial_rope(k.reshape(Bn, T, N_KV_HEADS, HEAD_DIM), pos)
    v = v.reshape(Bn, T, N_KV_HEADS, HEAD_DIM)
    u = gate_c * xc
    if past is None:
        attn = swa_prompt(q, k, v, sinks)
        k_all, v_all = k, v
        conv_prev = jnp.zeros((Bn, CONV_K - 1, CONV_WIDTH), a.dtype)
    else:
        k_prev, v_prev, conv_prev = past
        attn, k_all, v_all = swa_sample(q, k, v, k_prev, v_prev, sinks, pos)
    conv_y, conv_state = short_conv(u, conv_prev, conv_w)
    conv_out = gate_b * conv_y
    merged = jnp.concatenate([rms_norm(attn, g_attn_out), rms_norm(conv_out, g_conv_out)], axis=-1)
    out = jnp.einsum('bsc,cd->bsd', merged, w_out)
    return out, k_all[:, -WINDOW:], v_all[:, -WINDOW:], conv_state


def peer_ffn(x, w_q, sub_keys, expert_u, expert_v):
    shp = x.shape
    xt = x.reshape(-1, D_MODEL)
    n = xt.shape[0]
    nblk = -(-n // PEER_BLOCK)
    xt = jnp.pad(xt, ((0, nblk * PEER_BLOCK - n), (0, 0)))

    def block(xb):
        q = (xb @ w_q).reshape(PEER_BLOCK, PEER_HEADS, 2, PEER_HALF)
        s = jnp.einsum('thcd,hckd->thck', q, sub_keys).astype(jnp.float32)
        sv, si = lax.top_k(s, PEER_TOPK)
        cand = sv[:, :, 0, :, None] + sv[:, :, 1, None, :]
        cidx = si[:, :, 0, :, None] * N_KEYS + si[:, :, 1, None, :]
        cand = cand.reshape(PEER_BLOCK, PEER_HEADS, PEER_TOPK * PEER_TOPK)
        cidx = cidx.reshape(PEER_BLOCK, PEER_HEADS, PEER_TOPK * PEER_TOPK)
        top_s, top_pos = lax.top_k(cand, PEER_TOPK)
        eidx = jnp.take_along_axis(cidx, top_pos, axis=-1)
        gate = jax.nn.softmax(top_s, axis=-1)
        u_sel = jnp.take(expert_u, eidx, axis=0)
        hid = jax.nn.gelu(jnp.einsum('thkd,td->thk', u_sel, xb).astype(jnp.float32), approximate=False)
        v_sel = jnp.take(expert_v, eidx, axis=0)
        return jnp.einsum('thk,thkd->td', (gate * hid).astype(xb.dtype), v_sel)

    y = lax.map(block, xt.reshape(nblk, PEER_BLOCK, D_MODEL))
    return y.reshape(-1, D_MODEL)[:n].reshape(shp)


def run_trunk(x, p, pos, past_k, past_v, past_conv, g_mix, w_in, conv_w, attn_sinks,
              g_attn_out, g_conv_out, w_out, g_ffn, w_peer_q, peer_sub_keys, expert_u,
              expert_v, g_ple, w_ple_gate, w_ple, g_final):
    h = x
    ks, vs, cs = [], [], []
    for i in range(DEPTH):
        past = None if past_k is None else (past_k[i], past_v[i], past_conv[i])
        mix, kw, vw, cw = hybrid_mixer(rms_norm(h, g_mix[i]), pos, past, w_in[i], conv_w[i],
                                       attn_sinks[i], g_attn_out[i], g_conv_out[i], w_out[i])
        h = h + mix
        h = h + peer_ffn(rms_norm(h, g_ffn[i]), w_peer_q[i], peer_sub_keys[i], expert_u[i], expert_v[i])
        ple = jnp.einsum('bsp,pd->bsd', p[i], w_ple[i]).astype(jnp.float32)
        gate = jax.nn.sigmoid(jnp.einsum('bsd,de->bse', rms_norm(h, g_ple[i]), w_ple_gate[i]).astype(jnp.float32))
        h = h + (ple * gate).astype(h.dtype)
        ks.append(kw)
        vs.append(vw)
        cs.append(cw)
    return rms_norm(h, g_final), jnp.stack(ks), jnp.stack(vs), jnp.stack(cs)


def setup_inputs(seed: int = 0) -> dict:
    key = jax.random.key(seed)
    ks = jax.random.split(key, 24)
    f32 = jnp.float32

    def nrm(k, shape, scale=1.0):
        return jax.random.normal(k, shape, f32) * scale

    def gain(k, shape):
        return 1.0 + 0.01 * jax.random.normal(k, shape, f32)

    return {
        'x_prompt': nrm(ks[0], (BATCH, SEQ, D_MODEL)),
        'x_sample': nrm(ks[1], (DEC_BATCH, DEC_SEQ, D_MODEL)),
        'cache_k': nrm(ks[2], (DEPTH, DEC_BATCH, WINDOW, N_KV_HEADS, HEAD_DIM)),
        'cache_v': nrm(ks[3], (DEPTH, DEC_BATCH, WINDOW, N_KV_HEADS, HEAD_DIM)),
        'state_conv': nrm(ks[4], (DEPTH, DEC_BATCH, CONV_K - 1, CONV_WIDTH)),
        'p_prompt': nrm(ks[5], (DEPTH, BATCH, SEQ, PLE_DIM)),
        'p_sample': nrm(ks[6], (DEPTH, DEC_BATCH, DEC_SEQ, PLE_DIM)),
        'g_mix': gain(ks[7], (DEPTH, D_MODEL)),
        'w_in': nrm(ks[8], (DEPTH, D_MODEL, IN_COLS), D_MODEL ** -0.5),
        'conv_w': nrm(ks[9], (DEPTH, CONV_K, CONV_WIDTH), CONV_K ** -0.5),
        'attn_sinks': nrm(ks[10], (DEPTH, N_HEADS)),
        'g_attn_out': gain(ks[11], (DEPTH, ATTN_WIDTH)),
        'g_conv_out': gain(ks[12], (DEPTH, CONV_WIDTH)),
        'w_out': nrm(ks[13], (DEPTH, MIX_WIDTH, D_MODEL), MIX_WIDTH ** -0.5),
        'g_ffn': gain(ks[14], (DEPTH, D_MODEL)),
        'w_peer_q': nrm(ks[15], (DEPTH, D_MODEL, PEER_HEADS * PEER_QDIM), D_MODEL ** -0.5),
        'peer_sub_keys': nrm(ks[16], (DEPTH, PEER_HEADS, 2, N_KEYS, PEER_HALF), PEER_HALF ** -0.5),
        'expert_u': nrm(ks[17], (DEPTH, N_EXPERTS, D_MODEL), D_MODEL ** -0.5),
        'expert_v': nrm(ks[18], (DEPTH, N_EXPERTS, D_MODEL), 0.5),
        'g_ple': gain(ks[19], (DEPTH, D_MODEL)),
        'w_ple_gate': nrm(ks[20], (DEPTH, D_MODEL, D_MODEL), D_MODEL ** -0.5),
        'w_ple': nrm(ks[21], (DEPTH, PLE_DIM, D_MODEL), PLE_DIM ** -0.5),
        'g_final': gain(ks[22], (D_MODEL,)),
    }


def reference(x_prompt, x_sample, cache_k, cache_v, state_conv, p_prompt, p_sample,
              g_mix, w_in, conv_w, attn_sinks, g_attn_out, g_conv_out, w_out, g_ffn,
              w_peer_q, peer_sub_keys, expert_u, expert_v, g_ple, w_ple_gate, w_ple, g_final):
    pos_prompt = jnp.arange(x_prompt.shape[1], dtype=jnp.int32)
    pos_sample = PAST_LEN + jnp.arange(x_sample.shape[1], dtype=jnp.int32)
    y_prompt, new_k_prompt, new_v_prompt, new_conv_prompt = run_trunk(
        x_prompt, p_prompt, pos_prompt, None, None, None,
        g_mix, w_in, conv_w, attn_sinks, g_attn_out, g_conv_out, w_out, g_ffn,
        w_peer_q, peer_sub_keys, expert_u, expert_v, g_ple, w_ple_gate, w_ple, g_final)
    y_sample, new_k_sample, new_v_sample, new_conv_sample = run_trunk(
        x_sample, p_sample, pos_sample, cache_k, cache_v, state_conv,
        g_mix, w_in, conv_w, attn_sinks, g_attn_out, g_conv_out, w_out, g_ffn,
        w_peer_q, peer_sub_keys, expert_u, expert_v, g_ple, w_ple_gate, w_ple, g_final)
    return (y_prompt, y_sample, new_k_prompt, new_v_prompt, new_conv_prompt,
            new_k_sample, new_v_sample, new_conv_sample)
```

```python
import functools

import jax
import jax.numpy as jnp
from jax import lax
from jax.experimental import pallas as pl
from jax.experimental.pallas import tpu as pltpu

F32 = jnp.float32
BF16 = jnp.bfloat16

D_MODEL = 2048
HEAD_DIM = 64
N_HEADS = 16
N_KV_HEADS = 4
GROUP = N_HEADS // N_KV_HEADS
ATTN_WIDTH = N_HEADS * HEAD_DIM
KV_WIDTH = N_KV_HEADS * HEAD_DIM
CONV_WIDTH = D_MODEL - ATTN_WIDTH
CONV_K = 3
WINDOW = 128
ATTN_SCALE = HEAD_DIM ** -0.5
ROPE_THETA = 500000.0
ROPE_DIM = HEAD_DIM // 4
ROPE_HALF = ROPE_DIM // 2
IN_COLS = ATTN_WIDTH + 2 * KV_WIDTH + 3 * CONV_WIDTH
PAST_LEN = 16384
PEER_HEADS = 8
N_KEYS = 128
N_EXPERTS = N_KEYS * N_KEYS
PEER_TOPK = 16
PEER_QDIM = 256
PEER_HALF = PEER_QDIM // 2
PLE_DIM = 256
EPS = 1e-6
NEG = -1e30
NEG_INF = float("-inf")

LANES = 128
SUBLANES = 8
VMEM_LIMIT = 56 * 1024 * 1024


def _params(*sem):
    return pltpu.CompilerParams(dimension_semantics=sem, vmem_limit_bytes=VMEM_LIMIT)


def _resident(shape):
    zeros = (0,) * len(shape)
    return pl.BlockSpec(shape, lambda *_: zeros, pipeline_mode=pl.Buffered(1))


def _rms(x, g):
    ms = jnp.mean(x * x, axis=-1, keepdims=True)
    return (x * lax.rsqrt(ms + EPS)) * g


def _in_proj_kernel(x_ref, g_ref, w_ref, c_ref, sa_ref, sb_ref,
                    q_ref, k_ref, v_ref, gb_ref, u_ref):
    a = _rms(x_ref[...], g_ref[...])
    z = jnp.dot(a.astype(BF16), w_ref[...], preferred_element_type=F32)
    o_k = ATTN_WIDTH
    o_v = o_k + KV_WIDTH
    o_gb = o_v + KV_WIDTH
    o_gc = o_gb + CONV_WIDTH
    o_xc = o_gc + CONV_WIDTH

    def rope(t):
        width = t.shape[1]
        reps = width // LANES
        c = jnp.concatenate([c_ref[...]] * reps, axis=1)
        sa = jnp.concatenate([sa_ref[...]] * reps, axis=1)
        sb = jnp.concatenate([sb_ref[...]] * reps, axis=1)
        up = pltpu.roll(t, width - ROPE_HALF, axis=1)
        dn = pltpu.roll(t, ROPE_HALF, axis=1)
        return t * c + up * sa + dn * sb

    q_ref[...] = rope(z[:, :o_k])
    k_ref[...] = rope(z[:, o_k:o_v])
    v_ref[...] = z[:, o_v:o_gb]
    gb_ref[...] = z[:, o_gb:o_gc]
    u_ref[...] = z[:, o_gc:o_xc] * z[:, o_xc:]


def _rope_tables(pos):
    inv = ROPE_THETA ** (-jnp.arange(ROPE_HALF, dtype=F32) / ROPE_HALF)
    ang = pos.astype(F32)[:, None] * inv
    cos, sin = jnp.cos(ang), jnp.sin(ang)
    t = pos.shape[0]
    ones = jnp.ones((t, HEAD_DIM - ROPE_DIM), F32)
    zeros = jnp.zeros((t, HEAD_DIM - ROPE_DIM), F32)
    z8 = jnp.zeros((t, ROPE_HALF), F32)
    reps = LANES // HEAD_DIM
    c = jnp.concatenate([cos, cos, ones] * reps, axis=1)
    sa = jnp.concatenate([-sin, z8, zeros] * reps, axis=1)
    sb = jnp.concatenate([z8, sin, zeros] * reps, axis=1)
    return c, sa, sb


def _in_proj(x, g, w_bf, pos, tm):
    t = x.shape[0]
    c, sa, sb = _rope_tables(pos)
    row = lambda w: pl.BlockSpec((tm, w), lambda i: (i, 0))
    return pl.pallas_call(
        _in_proj_kernel,
        grid=(t // tm,),
        in_specs=[row(D_MODEL), _resident((1, D_MODEL)), _resident((D_MODEL, IN_COLS)),
                  row(LANES), row(LANES), row(LANES)],
        out_specs=[row(ATTN_WIDTH), row(KV_WIDTH), row(KV_WIDTH), row(CONV_WIDTH), row(CONV_WIDTH)],
        out_shape=[jax.ShapeDtypeStruct((t, ATTN_WIDTH), F32),
                   jax.ShapeDtypeStruct((t, KV_WIDTH), F32),
                   jax.ShapeDtypeStruct((t, KV_WIDTH), F32),
                   jax.ShapeDtypeStruct((t, CONV_WIDTH), F32),
                   jax.ShapeDtypeStruct((t, CONV_WIDTH), F32)],
        compiler_params=_params("parallel"),
        name="in_proj",
    )(x, g.reshape(1, D_MODEL), w_bf, c, sa, sb)


def _mix_tail(attn, conv_out, x, ga, gc, wout):
    merged = jnp.concatenate([_rms(attn, ga), _rms(conv_out, gc)], axis=1)
    return x + jnp.dot(merged.astype(BF16), wout, preferred_element_type=F32)


def _mixer_prompt_kernel(sinks_ref, q_ref, kp_ref, kc_ref, vp_ref, vc_ref, gb_ref, up_ref, uc_ref,
                         x_ref, wout_ref, convw_ref, ga_ref, gc_ref, h_ref):
    n = pl.program_id(0)
    q = q_ref[...]
    kband = jnp.concatenate([kp_ref[...], kc_ref[...]], axis=0).astype(BF16)
    vband = jnp.concatenate([vp_ref[...], vc_ref[...]], axis=0).astype(BF16)
    qi = lax.broadcasted_iota(jnp.int32, (WINDOW, 2 * WINDOW), 0) + WINDOW
    kj = lax.broadcasted_iota(jnp.int32, (WINDOW, 2 * WINDOW), 1)
    diff = qi - kj
    mask = jnp.where(diff >= 0, jnp.where(diff < WINDOW, jnp.where(kj + n * WINDOW >= WINDOW, 1, 0), 0), 0) > 0
    outs = []
    for h in range(N_HEADS):
        g = h // GROUP
        qh = q[:, h * HEAD_DIM:(h + 1) * HEAD_DIM].astype(BF16)
        kg = kband[:, g * HEAD_DIM:(g + 1) * HEAD_DIM]
        vg = vband[:, g * HEAD_DIM:(g + 1) * HEAD_DIM]
        s = lax.dot_general(qh, kg, (((1,), (1,)), ((), ())), preferred_element_type=F32) * ATTN_SCALE
        s = jnp.where(mask, s, NEG)
        sk = sinks_ref[h]
        m = jnp.maximum(jnp.max(s, axis=-1, keepdims=True), sk)
        e = jnp.where(mask, jnp.exp(s - m), 0.0)
        den = jnp.sum(e, axis=-1, keepdims=True) + jnp.exp(sk - m)
        p = e / den
        outs.append(jnp.dot(p.astype(BF16), vg, preferred_element_type=F32))
    attn = jnp.concatenate(outs, axis=1)

    u = uc_ref[...]
    prev = jnp.where(n > 0, up_ref[...], 0.0)
    ext = jnp.concatenate([prev, u], axis=0)
    cw = convw_ref[...]
    y = (ext[SUBLANES - 2:SUBLANES - 2 + WINDOW] * cw[0:1]
         + ext[SUBLANES - 1:SUBLANES - 1 + WINDOW] * cw[1:2]
         + u * cw[2:3])
    conv_out = gb_ref[...] * y
    h_ref[...] = _mix_tail(attn, conv_out, x_ref[...], ga_ref[...], gc_ref[...], wout_ref[...])


def _mixer_prompt(q, k, v, gb, u, x, wout_bf, conv_w, sinks, ga, gc):
    t = q.shape[0]
    nb = t // WINDOW
    cur = lambda w: pl.BlockSpec((WINDOW, w), lambda n: (n, 0))
    prv = lambda w: pl.BlockSpec((WINDOW, w), lambda n: (jnp.maximum(n - 1, 0), 0))
    rows_per_blk = WINDOW // SUBLANES
    tail = pl.BlockSpec((SUBLANES, CONV_WIDTH), lambda n: (jnp.maximum(n * rows_per_blk - 1, 0), 0))
    return pl.pallas_call(
        _mixer_prompt_kernel,
        grid=(nb,),
        in_specs=[pl.BlockSpec(memory_space=pltpu.SMEM),
                  cur(ATTN_WIDTH), prv(KV_WIDTH), cur(KV_WIDTH), prv(KV_WIDTH), cur(KV_WIDTH),
                  cur(CONV_WIDTH), tail, cur(CONV_WIDTH), cur(D_MODEL),
                  _resident((D_MODEL, D_MODEL)), _resident((SUBLANES, CONV_WIDTH)),
                  _resident((1, ATTN_WIDTH)), _resident((1, CONV_WIDTH))],
        out_specs=cur(D_MODEL),
        out_shape=jax.ShapeDtypeStruct((t, D_MODEL), F32),
        compiler_params=_params("parallel"),
        name="mixer_prompt",
    )(sinks, q, k, k, v, v, gb, u, u, x, wout_bf,
      jnp.pad(conv_w, ((0, SUBLANES - CONV_K), (0, 0))), ga.reshape(1, -1), gc.reshape(1, -1))


SAMPLE_ROWS = 8


def _attn_sample_kernel(q_ref, kn_ref, vn_ref, ck_ref, cv_ref, sk_ref, o_ref, nk_ref, nv_ref):
    hl = lax.broadcasted_iota(jnp.int32, (N_HEADS, KV_WIDTH), 1) // HEAD_DIM
    hg = lax.broadcasted_iota(jnp.int32, (N_HEADS, KV_WIDTH), 0) // GROUP
    own = hl == hg
    last = lax.broadcasted_iota(jnp.int32, (WINDOW, KV_WIDTH), 0) == WINDOW - 1
    sk = sk_ref[...][:, 0:1]
    kn = kn_ref[...]
    vn = vn_ref[...]
    for b in range(SAMPLE_ROWS):
        nk = jnp.where(last, kn[b:b + 1], pltpu.roll(ck_ref[b], WINDOW - 1, axis=0))
        nv = jnp.where(last, vn[b:b + 1], pltpu.roll(cv_ref[b], WINDOW - 1, axis=0))
        nk_ref[b] = nk
        nv_ref[b] = nv
        qb = q_ref[b]
        qw = jnp.where(own, jnp.concatenate([qb] * N_KV_HEADS, axis=1), 0.0)
        s = lax.dot_general(qw.astype(BF16), nk.astype(BF16), (((1,), (1,)), ((), ())),
                            preferred_element_type=F32) * ATTN_SCALE
        m = jnp.maximum(jnp.max(s, axis=-1, keepdims=True), sk)
        e = jnp.exp(s - m)
        p = e / (jnp.sum(e, axis=-1, keepdims=True) + jnp.exp(sk - m))
        pv = jnp.where(own, jnp.dot(p.astype(BF16), nv.astype(BF16), preferred_element_type=F32), 0.0)
        o = pv[:, 0:HEAD_DIM]
        for g in range(1, N_KV_HEADS):
            o = o + pv[:, g * HEAD_DIM:(g + 1) * HEAD_DIM]
        o_ref[b] = o


def _attn_sample(q, kn, vn, ck, cv, sinks):
    b = q.shape[0]
    q3 = q.reshape(b, N_HEADS, HEAD_DIM)
    sk = jnp.broadcast_to(sinks.reshape(N_HEADS, 1), (N_HEADS, LANES))
    r = SAMPLE_ROWS
    o3, nk, nv = pl.pallas_call(
        _attn_sample_kernel,
        grid=(b // r,),
        in_specs=[pl.BlockSpec((r, N_HEADS, HEAD_DIM), lambda i: (i, 0, 0)),
                  pl.BlockSpec((r, KV_WIDTH), lambda i: (i, 0)),
                  pl.BlockSpec((r, KV_WIDTH), lambda i: (i, 0)),
                  pl.BlockSpec((r, WINDOW, KV_WIDTH), lambda i: (i, 0, 0)),
                  pl.BlockSpec((r, WINDOW, KV_WIDTH), lambda i: (i, 0, 0)),
                  _resident((N_HEADS, LANES))],
        out_specs=[pl.BlockSpec((r, N_HEADS, HEAD_DIM), lambda i: (i, 0, 0)),
                   pl.BlockSpec((r, WINDOW, KV_WIDTH), lambda i: (i, 0, 0)),
                   pl.BlockSpec((r, WINDOW, KV_WIDTH), lambda i: (i, 0, 0))],
        out_shape=[jax.ShapeDtypeStruct((b, N_HEADS, HEAD_DIM), F32),
                   jax.ShapeDtypeStruct((b, WINDOW, KV_WIDTH), F32),
                   jax.ShapeDtypeStruct((b, WINDOW, KV_WIDTH), F32)],
        compiler_params=_params("parallel"),
        name="attn_sample",
    )(q3, kn, vn, ck, cv, sk)
    return o3.reshape(b, ATTN_WIDTH), nk, nv


def _mix_sample_kernel(attn_ref, gb_ref, u_ref, st_ref, x_ref, wout_ref, convw_ref, ga_ref, gc_ref,
                       h_ref, ns_ref):
    u = u_ref[...]
    st = st_ref[...]
    p0 = st[:, :CONV_WIDTH]
    p1 = st[:, CONV_WIDTH:]
    cw = convw_ref[...]
    y = p0 * cw[0:1] + p1 * cw[1:2] + u * cw[2:3]
    ns_ref[...] = jnp.concatenate([p1, u], axis=1)
    h_ref[...] = _mix_tail(attn_ref[...], gb_ref[...] * y, x_ref[...], ga_ref[...], gc_ref[...],
                           wout_ref[...])


def _mix_sample(attn, gb, u, state2, x, wout_bf, conv_w, ga, gc):
    b = attn.shape[0]
    full = lambda w: pl.BlockSpec((b, w), lambda i: (0, 0))
    return pl.pallas_call(
        _mix_sample_kernel,
        grid=(1,),
        in_specs=[full(ATTN_WIDTH), full(CONV_WIDTH), full(CONV_WIDTH), full(2 * CONV_WIDTH),
                  full(D_MODEL), _resident((D_MODEL, D_MODEL)), _resident((SUBLANES, CONV_WIDTH)),
                  _resident((1, ATTN_WIDTH)), _resident((1, CONV_WIDTH))],
        out_specs=[full(D_MODEL), full(2 * CONV_WIDTH)],
        out_shape=[jax.ShapeDtypeStruct((b, D_MODEL), F32),
                   jax.ShapeDtypeStruct((b, 2 * CONV_WIDTH), F32)],
        compiler_params=_params("arbitrary"),
        name="mix_sample",
    )(attn, gb, u, state2, x, wout_bf,
      jnp.pad(conv_w, ((0, SUBLANES - CONV_K), (0, 0))), ga.reshape(1, -1), gc.reshape(1, -1))


def _peer_query_kernel(h_ref, g_ref, w_ref, q_ref, xb_ref):
    a = _rms(h_ref[...], g_ref[...]).astype(BF16)
    xb_ref[...] = a
    q_ref[...] = jnp.dot(a, w_ref[...], preferred_element_type=F32)


def _peer_query(h, g, wq_bf, tm):
    t = h.shape[0]
    row = lambda w: pl.BlockSpec((tm, w), lambda i: (i, 0))
    return pl.pallas_call(
        _peer_query_kernel,
        grid=(t // tm,),
        in_specs=[row(D_MODEL), _resident((1, D_MODEL)), _resident((D_MODEL, PEER_HEADS * PEER_QDIM))],
        out_specs=[row(PEER_HEADS * PEER_QDIM), row(D_MODEL)],
        out_shape=[jax.ShapeDtypeStruct((t, PEER_HEADS * PEER_QDIM), F32),
                   jax.ShapeDtypeStruct((t, D_MODEL), BF16)],
        compiler_params=_params("parallel"),
        name="peer_query",
    )(h, g.reshape(1, D_MODEL), wq_bf)


def _top_sorted(s):
    rows = lax.broadcasted_iota(jnp.int32, (PEER_TOPK, s.shape[1]), 0)
    top = jnp.zeros((PEER_TOPK, s.shape[1]), F32)
    work = s
    for r in range(PEER_TOPK):
        m = jnp.max(work, axis=0, keepdims=True)
        top = jnp.where(rows == r, m, top)
        work = jnp.where(work >= m, NEG_INF, work)
    return top


def _peer_select_kernel(q_ref, keys_ref, s1_ref, a_ref, s2_ref, b_ref, tau_ref):
    tb = q_ref.shape[0]
    q = q_ref[...].astype(BF16)
    taus = []
    for h in range(PEER_HEADS):
        sc = []
        for c in range(2):
            col = (2 * h + c) * PEER_HALF
            sc.append(lax.dot_general(keys_ref[h, c], q[:, col:col + PEER_HALF],
                                      (((1,), (1,)), ((), ())), preferred_element_type=F32))
        s1, s2 = sc
        t1 = _top_sorted(s1)
        t2 = _top_sorted(s2)
        r16 = lax.broadcasted_iota(jnp.int32, (16, tb), 0)
        r8 = lax.broadcasted_iota(jnp.int32, (8, tb), 0)
        pieces = [t1[0:1] + t2]
        for r in range(1, 5):
            pieces.append(t1[r:r + 1] + t2[0:8])
        pieces.append(jnp.where(r16 >= 5, t1 + t2[0:1], NEG_INF))
        pieces.append(jnp.where(r8 >= 5, t1[0:8] + t2[1:2], NEG_INF))
        cand = jnp.concatenate(pieces, axis=0)
        work = cand
        for _ in range(PEER_TOPK):
            tau = jnp.max(work, axis=0, keepdims=True)
            work = jnp.where(work >= tau, NEG_INF, work)
        top = t1[0:1] + t2[0:1]
        z = jnp.sum(jnp.where(cand >= tau, jnp.exp(cand - top), 0.0), axis=0, keepdims=True)
        s1_ref[h] = s1
        s2_ref[h] = s2
        a_ref[h] = jnp.exp(s1 - t1[0:1]) / z
        b_ref[h] = jnp.exp(s2 - t2[0:1])
        taus.append(tau)
    tau_ref[...] = jnp.concatenate(taus, axis=0)


def _peer_select(q, keys_bf, tb):
    t = q.shape[0]
    big = pl.BlockSpec((PEER_HEADS, N_KEYS, tb), lambda i: (0, 0, i))
    big_shape = jax.ShapeDtypeStruct((PEER_HEADS, N_KEYS, t), F32)
    return pl.pallas_call(
        _peer_select_kernel,
        grid=(t // tb,),
        in_specs=[pl.BlockSpec((tb, PEER_HEADS * PEER_QDIM), lambda i: (i, 0)),
                  _resident((PEER_HEADS, 2, N_KEYS, PEER_HALF))],
        out_specs=[big, big, big, big, pl.BlockSpec((PEER_HEADS, tb), lambda i: (0, i))],
        out_shape=[big_shape, big_shape, big_shape, big_shape,
                   jax.ShapeDtypeStruct((PEER_HEADS, t), F32)],
        compiler_params=_params("parallel"),
        name="peer_select",
    )(q, keys_bf)


EXPERT_CHUNK = 1024
KEY_ROWS = EXPERT_CHUNK // N_KEYS


def _gelu(x):
    return 0.5 * x * (1.0 + lax.erf(x * (2.0 ** -0.5)))


def _peer_experts_kernel(xb_ref, u_ref, vt_ref, s1_ref, a_ref, s2_ref, b_ref, tau_ref,
                         o_ref, acc_ref, g_ref):
    e = pl.program_id(1)

    @pl.when(e == 0)
    def _():
        acc_ref[...] = jnp.zeros_like(acc_ref)

    ht = lax.dot_general(u_ref[...], xb_ref[...], (((1,), (1,)), ((), ())),
                         preferred_element_type=F32)
    for r in range(KEY_ROWS):
        w = None
        for h in range(PEER_HEADS):
            pair = s2_ref[h] + s1_ref[h, r:r + 1, :]
            term = jnp.where(pair >= tau_ref[h:h + 1, :], b_ref[h] * a_ref[h, r:r + 1, :], 0.0)
            w = term if w is None else w + term
        g = _gelu(ht[r * N_KEYS:(r + 1) * N_KEYS]) * w
        g_ref[r * N_KEYS:(r + 1) * N_KEYS, :] = g.astype(BF16)
    acc_ref[...] += jnp.dot(vt_ref[...], g_ref[...], preferred_element_type=F32)

    @pl.when(e == pl.num_programs(1) - 1)
    def _():
        o_ref[...] = acc_ref[...].T


def _peer_experts(xb, u_bf, vt_bf, s1, a, s2, b, tau, tb):
    t = xb.shape[0]
    rows = pl.BlockSpec((PEER_HEADS, KEY_ROWS, tb), lambda i, e: (0, e, i))
    full = pl.BlockSpec((PEER_HEADS, N_KEYS, tb), lambda i, e: (0, 0, i))
    return pl.pallas_call(
        _peer_experts_kernel,
        grid=(t // tb, N_EXPERTS // EXPERT_CHUNK),
        in_specs=[pl.BlockSpec((tb, D_MODEL), lambda i, e: (i, 0)),
                  pl.BlockSpec((EXPERT_CHUNK, D_MODEL), lambda i, e: (e, 0)),
                  pl.BlockSpec((D_MODEL, EXPERT_CHUNK), lambda i, e: (0, e)),
                  rows, rows, full, full,
                  pl.BlockSpec((PEER_HEADS, tb), lambda i, e: (0, i))],
        out_specs=pl.BlockSpec((tb, D_MODEL), lambda i, e: (i, 0)),
        out_shape=jax.ShapeDtypeStruct((t, D_MODEL), F32),
        scratch_shapes=[pltpu.VMEM((D_MODEL, tb), F32), pltpu.VMEM((EXPERT_CHUNK, tb), BF16)],
        compiler_params=_params("parallel", "arbitrary"),
        name="peer_experts",
    )(xb, u_bf, vt_bf, s1, a, s2, b, tau)


def _final_kernel(h_ref, y_ref, p_ref, gp_ref, wg_ref, wp_ref, gf_ref, o_ref):
    h = h_ref[...] + y_ref[...]
    ple = jnp.dot(p_ref[...].astype(BF16), wp_ref[...], preferred_element_type=F32)
    gate = jax.nn.sigmoid(jnp.dot(_rms(h, gp_ref[...]).astype(BF16), wg_ref[...],
                                  preferred_element_type=F32))
    o_ref[...] = _rms(h + ple * gate, gf_ref[...])


def _final(h, y, p, g_ple, wg_bf, wp_bf, g_final, tm):
    t = h.shape[0]
    row = lambda w: pl.BlockSpec((tm, w), lambda i: (i, 0))
    return pl.pallas_call(
        _final_kernel,
        grid=(t // tm,),
        in_specs=[row(D_MODEL), row(D_MODEL), row(PLE_DIM), _resident((1, D_MODEL)),
                  _resident((D_MODEL, D_MODEL)), _resident((PLE_DIM, D_MODEL)), _resident((1, D_MODEL))],
        out_specs=row(D_MODEL),
        out_shape=jax.ShapeDtypeStruct((t, D_MODEL), F32),
        compiler_params=_params("parallel"),
        name="final",
    )(h, y, p, g_ple.reshape(1, D_MODEL), wg_bf, wp_bf, g_final.reshape(1, D_MODEL))


def _tile(t, pref):
    return pref if t % pref == 0 else t


def _channel_mix(h1, p, w, tm, tb):
    q, xb = _peer_query(h1, w["g_ffn"], w["wq"], tm)
    s1, a, s2, b, tau = _peer_select(q, w["keys"], tb)
    y = _peer_experts(xb, w["u"], w["vt"], s1, a, s2, b, tau, tb)
    return _final(h1, y, p, w["g_ple"], w["wg"], w["wp"], w["g_final"], tm)


def kernel(x_prompt, x_sample, cache_k, cache_v, state_conv, p_prompt, p_sample, g_mix, w_in, conv_w,
           attn_sinks, g_attn_out, g_conv_out, w_out, g_ffn, w_peer_q, peer_sub_keys, expert_u,
           expert_v, g_ple, w_ple_gate, w_ple, g_final):
    seq = x_prompt.shape[1]
    nb = x_sample.shape[0]
    w = {
        "g_ffn": g_ffn[0], "wq": w_peer_q[0].astype(BF16), "keys": peer_sub_keys[0].astype(BF16),
        "u": expert_u[0].astype(BF16), "vt": expert_v[0].astype(BF16).T,
        "g_ple": g_ple[0], "wg": w_ple_gate[0].astype(BF16), "wp": w_ple[0].astype(BF16),
        "g_final": g_final,
    }
    win_bf = w_in[0].astype(BF16)
    wout_bf = w_out[0].astype(BF16)
    sinks = attn_sinks[0]

    xp = x_prompt[0]
    tm = _tile(seq, 256)
    tb = _tile(seq, 512)
    q, k, v, gb, u = _in_proj(xp, g_mix[0], win_bf, jnp.arange(seq, dtype=jnp.int32), tm)
    h1 = _mixer_prompt(q, k, v, gb, u, xp, wout_bf, conv_w[0], sinks, g_attn_out[0], g_conv_out[0])
    y_prompt = _channel_mix(h1, p_prompt[0, 0], w, tm, tb)[None]
    new_k_prompt = k[seq - WINDOW:].reshape(1, 1, WINDOW, N_KV_HEADS, HEAD_DIM)
    new_v_prompt = v[seq - WINDOW:].reshape(1, 1, WINDOW, N_KV_HEADS, HEAD_DIM)
    new_conv_prompt = u[seq - (CONV_K - 1):].reshape(1, 1, CONV_K - 1, CONV_WIDTH)

    xs = x_sample[:, 0]
    pos_s = jnp.full((nb,), PAST_LEN, jnp.int32)
    qs, ks, vs, gbs, us = _in_proj(xs, g_mix[0], win_bf, pos_s, nb)
    attn_s, nk, nv = _attn_sample(qs, ks, vs, cache_k[0].reshape(nb, WINDOW, KV_WIDTH),
                                  cache_v[0].reshape(nb, WINDOW, KV_WIDTH), sinks)
    h1s, ns = _mix_sample(attn_s, gbs, us, state_conv[0].reshape(nb, (CONV_K - 1) * CONV_WIDTH), xs,
                          wout_bf, conv_w[0], g_attn_out[0], g_conv_out[0])
    y_sample = _channel_mix(h1s, p_sample[0, :, 0], w, nb, nb)[:, None]
    new_k_sample = nk.reshape(1, nb, WINDOW, N_KV_HEADS, HEAD_DIM)
    new_v_sample = nv.reshape(1, nb, WINDOW, N_KV_HEADS, HEAD_DIM)
    new_conv_sample = ns.reshape(1, nb, CONV_K - 1, CONV_WIDTH)

    return (y_prompt, y_sample, new_k_prompt, new_v_prompt, new_conv_prompt,
            new_k_sample, new_v_sample, new_conv_sample)
```

```python
import functools

import jax
import jax.numpy as jnp
from jax import lax
from jax.experimental import pallas as pl
from jax.experimental.pallas import tpu as pltpu

F32 = jnp.float32
BF16 = jnp.bfloat16

D_MODEL = 2048
HEAD_DIM = 64
N_HEADS = 16
N_KV_HEADS = 4
GROUP = N_HEADS // N_KV_HEADS
ATTN_WIDTH = N_HEADS * HEAD_DIM
KV_WIDTH = N_KV_HEADS * HEAD_DIM
CONV_WIDTH = D_MODEL - ATTN_WIDTH
CONV_K = 3
WINDOW = 128
ATTN_SCALE = HEAD_DIM ** -0.5
ROPE_THETA = 500000.0
ROPE_DIM = HEAD_DIM // 4
ROPE_HALF = ROPE_DIM // 2
IN_COLS = ATTN_WIDTH + 2 * KV_WIDTH + 3 * CONV_WIDTH
PAST_LEN = 16384
PEER_HEADS = 8
N_KEYS = 128
N_EXPERTS = N_KEYS * N_KEYS
PEER_TOPK = 16
PEER_QDIM = 256
PEER_HALF = PEER_QDIM // 2
PLE_DIM = 256
EPS = 1e-6
NEG = -1e30
NEG_INF = float("-inf")

LANES = 128
SUBLANES = 8
MXU_DIM = 256
VMEM_LIMIT = 56 * 1024 * 1024


def _params(*sem, flags=None):
    return pltpu.CompilerParams(dimension_semantics=sem, vmem_limit_bytes=VMEM_LIMIT, flags=flags)


def _resident(shape):
    zeros = (0,) * len(shape)
    return pl.BlockSpec(shape, lambda *_: zeros, pipeline_mode=pl.Buffered(1))


def _rms(x, g):
    ms = jnp.mean(x * x, axis=-1, keepdims=True)
    return (x * lax.rsqrt(ms + EPS)) * g


def _in_proj_kernel(x_ref, g_ref, w_ref, c_ref, sa_ref, sb_ref,
                    q_ref, k_ref, v_ref, gb_ref, u_ref):
    a = _rms(x_ref[...], g_ref[...])
    z = jnp.dot(a.astype(BF16), w_ref[...], preferred_element_type=F32)
    o_k = ATTN_WIDTH
    o_v = o_k + KV_WIDTH
    o_gb = o_v + KV_WIDTH
    o_gc = o_gb + CONV_WIDTH
    o_xc = o_gc + CONV_WIDTH

    def rope(t):
        width = t.shape[1]
        reps = width // LANES
        c = jnp.concatenate([c_ref[...]] * reps, axis=1)
        sa = jnp.concatenate([sa_ref[...]] * reps, axis=1)
        sb = jnp.concatenate([sb_ref[...]] * reps, axis=1)
        up = pltpu.roll(t, width - ROPE_HALF, axis=1)
        dn = pltpu.roll(t, ROPE_HALF, axis=1)
        return t * c + up * sa + dn * sb

    q_ref[...] = rope(z[:, :o_k])
    k_ref[...] = rope(z[:, o_k:o_v])
    v_ref[...] = z[:, o_v:o_gb]
    gb_ref[...] = z[:, o_gb:o_gc]
    u_ref[...] = z[:, o_gc:o_xc] * z[:, o_xc:]


def _rope_tables(pos):
    inv = ROPE_THETA ** (-jnp.arange(ROPE_HALF, dtype=F32) / ROPE_HALF)
    ang = pos.astype(F32)[:, None] * inv
    cos, sin = jnp.cos(ang), jnp.sin(ang)
    t = pos.shape[0]
    ones = jnp.ones((t, HEAD_DIM - ROPE_DIM), F32)
    zeros = jnp.zeros((t, HEAD_DIM - ROPE_DIM), F32)
    z8 = jnp.zeros((t, ROPE_HALF), F32)
    reps = LANES // HEAD_DIM
    c = jnp.concatenate([cos, cos, ones] * reps, axis=1)
    sa = jnp.concatenate([-sin, z8, zeros] * reps, axis=1)
    sb = jnp.concatenate([z8, sin, zeros] * reps, axis=1)
    return c, sa, sb


def _in_proj(x, g, w_bf, pos, tm):
    t = x.shape[0]
    c, sa, sb = _rope_tables(pos)
    row = lambda w: pl.BlockSpec((tm, w), lambda i: (i, 0))
    return pl.pallas_call(
        _in_proj_kernel,
        grid=(t // tm,),
        in_specs=[row(D_MODEL), _resident((1, D_MODEL)), _resident((D_MODEL, IN_COLS)),
                  row(LANES), row(LANES), row(LANES)],
        out_specs=[row(ATTN_WIDTH), row(KV_WIDTH), row(KV_WIDTH), row(CONV_WIDTH), row(CONV_WIDTH)],
        out_shape=[jax.ShapeDtypeStruct((t, ATTN_WIDTH), F32),
                   jax.ShapeDtypeStruct((t, KV_WIDTH), F32),
                   jax.ShapeDtypeStruct((t, KV_WIDTH), F32),
                   jax.ShapeDtypeStruct((t, CONV_WIDTH), F32),
                   jax.ShapeDtypeStruct((t, CONV_WIDTH), F32)],
        compiler_params=_params("parallel"),
        name="in_proj",
    )(x, g.reshape(1, D_MODEL), w_bf, c, sa, sb)


def _mix_tail(attn, conv_out, x, ga, gc, wout):
    merged = jnp.concatenate([_rms(attn, ga), _rms(conv_out, gc)], axis=1)
    return x + jnp.dot(merged.astype(BF16), wout, preferred_element_type=F32)


def _mixer_prompt_kernel(sinks_ref, q_ref, kp_ref, kc_ref, vp_ref, vc_ref, gb_ref, up_ref, uc_ref,
                         x_ref, wout_ref, convw_ref, ga_ref, gc_ref, h_ref):
    n = pl.program_id(0)
    q = q_ref[...]
    kband = jnp.concatenate([kp_ref[...], kc_ref[...]], axis=0).astype(BF16)
    vband = jnp.concatenate([vp_ref[...], vc_ref[...]], axis=0).astype(BF16)
    qi = lax.broadcasted_iota(jnp.int32, (WINDOW, 2 * WINDOW), 0) + WINDOW
    kj = lax.broadcasted_iota(jnp.int32, (WINDOW, 2 * WINDOW), 1)
    diff = qi - kj
    mask = jnp.where(diff >= 0, jnp.where(diff < WINDOW, jnp.where(kj + n * WINDOW >= WINDOW, 1, 0), 0), 0) > 0
    outs = []
    for h in range(N_HEADS):
        g = h // GROUP
        qh = q[:, h * HEAD_DIM:(h + 1) * HEAD_DIM].astype(BF16)
        kg = kband[:, g * HEAD_DIM:(g + 1) * HEAD_DIM]
        vg = vband[:, g * HEAD_DIM:(g + 1) * HEAD_DIM]
        s = lax.dot_general(qh, kg, (((1,), (1,)), ((), ())), preferred_element_type=F32) * ATTN_SCALE
        s = jnp.where(mask, s, NEG)
        sk = sinks_ref[h]
        m = jnp.maximum(jnp.max(s, axis=-1, keepdims=True), sk)
        e = jnp.where(mask, jnp.exp(s - m), 0.0)
        den = jnp.sum(e, axis=-1, keepdims=True) + jnp.exp(sk - m)
        p = e / den
        outs.append(jnp.dot(p.astype(BF16), vg, preferred_element_type=F32))
    attn = jnp.concatenate(outs, axis=1)

    u = uc_ref[...]
    prev = jnp.where(n > 0, up_ref[...], 0.0)
    ext = jnp.concatenate([prev, u], axis=0)
    cw = convw_ref[...]
    y = (ext[SUBLANES - 2:SUBLANES - 2 + WINDOW] * cw[0:1]
         + ext[SUBLANES - 1:SUBLANES - 1 + WINDOW] * cw[1:2]
         + u * cw[2:3])
    conv_out = gb_ref[...] * y
    h_ref[...] = _mix_tail(attn, conv_out, x_ref[...], ga_ref[...], gc_ref[...], wout_ref[...])


def _mixer_prompt(q, k, v, gb, u, x, wout_bf, conv_w, sinks, ga, gc):
    t = q.shape[0]
    nb = t // WINDOW
    cur = lambda w: pl.BlockSpec((WINDOW, w), lambda n: (n, 0))
    prv = lambda w: pl.BlockSpec((WINDOW, w), lambda n: (jnp.maximum(n - 1, 0), 0))
    rows_per_blk = WINDOW // SUBLANES
    tail = pl.BlockSpec((SUBLANES, CONV_WIDTH), lambda n: (jnp.maximum(n * rows_per_blk - 1, 0), 0))
    return pl.pallas_call(
        _mixer_prompt_kernel,
        grid=(nb,),
        in_specs=[pl.BlockSpec(memory_space=pltpu.SMEM),
                  cur(ATTN_WIDTH), prv(KV_WIDTH), cur(KV_WIDTH), prv(KV_WIDTH), cur(KV_WIDTH),
                  cur(CONV_WIDTH), tail, cur(CONV_WIDTH), cur(D_MODEL),
                  _resident((D_MODEL, D_MODEL)), _resident((SUBLANES, CONV_WIDTH)),
                  _resident((1, ATTN_WIDTH)), _resident((1, CONV_WIDTH))],
        out_specs=cur(D_MODEL),
        out_shape=jax.ShapeDtypeStruct((t, D_MODEL), F32),
        compiler_params=_params("parallel"),
        name="mixer_prompt",
    )(sinks, q, k, k, v, v, gb, u, u, x, wout_bf,
      jnp.pad(conv_w, ((0, SUBLANES - CONV_K), (0, 0))), ga.reshape(1, -1), gc.reshape(1, -1))


SAMPLE_ROWS = 8


def _attn_sample_kernel(q_ref, kn_ref, vn_ref, ck_ref, cv_ref, sk_ref, o_ref, nk_ref, nv_ref):
    hl = lax.broadcasted_iota(jnp.int32, (N_HEADS, KV_WIDTH), 1) // HEAD_DIM
    hg = lax.broadcasted_iota(jnp.int32, (N_HEADS, KV_WIDTH), 0) // GROUP
    own = hl == hg
    last = lax.broadcasted_iota(jnp.int32, (WINDOW, KV_WIDTH), 0) == WINDOW - 1
    sk = sk_ref[...][:, 0:1]
    kn = kn_ref[...]
    vn = vn_ref[...]
    for b in range(SAMPLE_ROWS):
        nk = jnp.where(last, kn[b:b + 1], pltpu.roll(ck_ref[b], WINDOW - 1, axis=0))
        nv = jnp.where(last, vn[b:b + 1], pltpu.roll(cv_ref[b], WINDOW - 1, axis=0))
        nk_ref[b] = nk
        nv_ref[b] = nv
        qb = q_ref[b]
        qw = jnp.where(own, jnp.concatenate([qb] * N_KV_HEADS, axis=1), 0.0)
        s = lax.dot_general(qw.astype(BF16), nk.astype(BF16), (((1,), (1,)), ((), ())),
                            preferred_element_type=F32) * ATTN_SCALE
        m = jnp.maximum(jnp.max(s, axis=-1, keepdims=True), sk)
        e = jnp.exp(s - m)
        p = e / (jnp.sum(e, axis=-1, keepdims=True) + jnp.exp(sk - m))
        pv = jnp.where(own, jnp.dot(p.astype(BF16), nv.astype(BF16), preferred_element_type=F32), 0.0)
        o = pv[:, 0:HEAD_DIM]
        for g in range(1, N_KV_HEADS):
            o = o + pv[:, g * HEAD_DIM:(g + 1) * HEAD_DIM]
        o_ref[b] = o


def _attn_sample(q, kn, vn, ck, cv, sinks):
    b = q.shape[0]
    q3 = q.reshape(b, N_HEADS, HEAD_DIM)
    sk = jnp.broadcast_to(sinks.reshape(N_HEADS, 1), (N_HEADS, LANES))
    r = SAMPLE_ROWS
    o3, nk, nv = pl.pallas_call(
        _attn_sample_kernel,
        grid=(b // r,),
        in_specs=[pl.BlockSpec((r, N_HEADS, HEAD_DIM), lambda i: (i, 0, 0)),
                  pl.BlockSpec((r, KV_WIDTH), lambda i: (i, 0)),
                  pl.BlockSpec((r, KV_WIDTH), lambda i: (i, 0)),
                  pl.BlockSpec((r, WINDOW, KV_WIDTH), lambda i: (i, 0, 0)),
                  pl.BlockSpec((r, WINDOW, KV_WIDTH), lambda i: (i, 0, 0)),
                  _resident((N_HEADS, LANES))],
        out_specs=[pl.BlockSpec((r, N_HEADS, HEAD_DIM), lambda i: (i, 0, 0)),
                   pl.BlockSpec((r, WINDOW, KV_WIDTH), lambda i: (i, 0, 0)),
                   pl.BlockSpec((r, WINDOW, KV_WIDTH), lambda i: (i, 0, 0))],
        out_shape=[jax.ShapeDtypeStruct((b, N_HEADS, HEAD_DIM), F32),
                   jax.ShapeDtypeStruct((b, WINDOW, KV_WIDTH), F32),
                   jax.ShapeDtypeStruct((b, WINDOW, KV_WIDTH), F32)],
        compiler_params=_params("parallel"),
        name="attn_sample",
    )(q3, kn, vn, ck, cv, sk)
    return o3.reshape(b, ATTN_WIDTH), nk, nv


def _mix_sample_kernel(attn_ref, gb_ref, u_ref, st_ref, x_ref, wout_ref, convw_ref, ga_ref, gc_ref,
                       h_ref, ns_ref):
    u = u_ref[...]
    st = st_ref[...]
    p0 = st[:, :CONV_WIDTH]
    p1 = st[:, CONV_WIDTH:]
    cw = convw_ref[...]
    y = p0 * cw[0:1] + p1 * cw[1:2] + u * cw[2:3]
    ns_ref[...] = jnp.concatenate([p1, u], axis=1)
    h_ref[...] = _mix_tail(attn_ref[...], gb_ref[...] * y, x_ref[...], ga_ref[...], gc_ref[...],
                           wout_ref[...])


def _mix_sample(attn, gb, u, state2, x, wout_bf, conv_w, ga, gc):
    b = attn.shape[0]
    full = lambda w: pl.BlockSpec((b, w), lambda i: (0, 0))
    return pl.pallas_call(
        _mix_sample_kernel,
        grid=(1,),
        in_specs=[full(ATTN_WIDTH), full(CONV_WIDTH), full(CONV_WIDTH), full(2 * CONV_WIDTH),
                  full(D_MODEL), _resident((D_MODEL, D_MODEL)), _resident((SUBLANES, CONV_WIDTH)),
                  _resident((1, ATTN_WIDTH)), _resident((1, CONV_WIDTH))],
        out_specs=[full(D_MODEL), full(2 * CONV_WIDTH)],
        out_shape=[jax.ShapeDtypeStruct((b, D_MODEL), F32),
                   jax.ShapeDtypeStruct((b, 2 * CONV_WIDTH), F32)],
        compiler_params=_params("arbitrary"),
        name="mix_sample",
    )(attn, gb, u, state2, x, wout_bf,
      jnp.pad(conv_w, ((0, SUBLANES - CONV_K), (0, 0))), ga.reshape(1, -1), gc.reshape(1, -1))


def _peer_query_kernel(h_ref, g_ref, w_ref, q_ref, xbt_ref):
    a = _rms(h_ref[...], g_ref[...])
    xbt_ref[...] = a.T.astype(BF16)
    q_ref[...] = jnp.dot(a.astype(BF16), w_ref[...], preferred_element_type=F32)


def _peer_query(h, g, wq_bf, tm):
    t = h.shape[0]
    row = lambda w: pl.BlockSpec((tm, w), lambda i: (i, 0))
    return pl.pallas_call(
        _peer_query_kernel,
        grid=(t // tm,),
        in_specs=[row(D_MODEL), _resident((1, D_MODEL)), _resident((D_MODEL, PEER_HEADS * PEER_QDIM))],
        out_specs=[row(PEER_HEADS * PEER_QDIM), pl.BlockSpec((D_MODEL, tm), lambda i: (0, i))],
        out_shape=[jax.ShapeDtypeStruct((t, PEER_HEADS * PEER_QDIM), F32),
                   jax.ShapeDtypeStruct((D_MODEL, t), BF16)],
        compiler_params=_params("parallel"),
        name="peer_query",
    )(h, g.reshape(1, D_MODEL), wq_bf)


def _top_sorted(s):
    rows = lax.broadcasted_iota(jnp.int32, (PEER_TOPK, s.shape[1]), 0)
    top = jnp.zeros((PEER_TOPK, s.shape[1]), F32)
    work = s
    for r in range(PEER_TOPK):
        m = jnp.max(work, axis=0, keepdims=True)
        top = jnp.where(rows == r, m, top)
        work = jnp.where(work >= m, NEG_INF, work)
    return top


def _peer_select_kernel(q_ref, keys_ref, s1_ref, a_ref, s2_ref, b_ref, tau_ref):
    tb = q_ref.shape[0]
    q = q_ref[...].astype(BF16)
    taus = []
    for h in range(PEER_HEADS):
        sc = []
        for c in range(2):
            col = (2 * h + c) * PEER_HALF
            sc.append(lax.dot_general(keys_ref[h, c], q[:, col:col + PEER_HALF],
                                      (((1,), (1,)), ((), ())), preferred_element_type=F32))
        s1, s2 = sc
        t1 = _top_sorted(s1)
        t2 = _top_sorted(s2)
        r16 = lax.broadcasted_iota(jnp.int32, (16, tb), 0)
        r8 = lax.broadcasted_iota(jnp.int32, (8, tb), 0)
        pieces = [t1[0:1] + t2]
        for r in range(1, 5):
            pieces.append(t1[r:r + 1] + t2[0:8])
        pieces.append(jnp.where(r16 >= 5, t1 + t2[0:1], NEG_INF))
        pieces.append(jnp.where(r8 >= 5, t1[0:8] + t2[1:2], NEG_INF))
        cand = jnp.concatenate(pieces, axis=0)
        work = cand
        for _ in range(PEER_TOPK):
            tau = jnp.max(work, axis=0, keepdims=True)
            work = jnp.where(work >= tau, NEG_INF, work)
        top = t1[0:1] + t2[0:1]
        z = jnp.sum(jnp.where(cand >= tau, jnp.exp(cand - top), 0.0), axis=0, keepdims=True)
        s1_ref[h] = s1
        s2_ref[h] = s2
        a_ref[h] = jnp.exp(s1 - t1[0:1]) / z
        b_ref[h] = jnp.exp(s2 - t2[0:1])
        taus.append(tau)
    tau_ref[...] = jnp.concatenate(taus, axis=0)


def _peer_select(q, keys_bf, tb):
    t = q.shape[0]
    big = pl.BlockSpec((PEER_HEADS, N_KEYS, tb), lambda i: (0, 0, i))
    big_shape = jax.ShapeDtypeStruct((PEER_HEADS, N_KEYS, t), F32)
    return pl.pallas_call(
        _peer_select_kernel,
        grid=(t // tb,),
        in_specs=[pl.BlockSpec((tb, PEER_HEADS * PEER_QDIM), lambda i: (i, 0)),
                  _resident((PEER_HEADS, 2, N_KEYS, PEER_HALF))],
        out_specs=[big, big, big, big, pl.BlockSpec((PEER_HEADS, tb), lambda i: (0, i))],
        out_shape=[big_shape, big_shape, big_shape, big_shape,
                   jax.ShapeDtypeStruct((PEER_HEADS, t), F32)],
        compiler_params=_params("parallel"),
        name="peer_select",
    )(q, keys_bf)


EXPERT_CHUNK = 1024
KEY_ROWS = EXPERT_CHUNK // N_KEYS


def _gelu(x):
    return 0.5 * x * (1.0 + lax.erf(x * (2.0 ** -0.5)))


HALF_CHUNK = EXPERT_CHUNK // 2
HALF_ROWS = KEY_ROWS // 2


def _peer_experts_kernel(u_ref, xbt_ref, vt_ref, s1_ref, a_ref, s2_ref, b_ref, tau_ref,
                         o_ref, acc_ref, ht0_ref, ht1_ref, g_ref, *, n_chunks):
    s = pl.program_id(0)
    e = lax.rem(jnp.maximum(s - 1, 0), n_chunks)

    @pl.when(s == 0)
    def _():
        ht1_ref[...] = jnp.zeros_like(ht1_ref)

    @pl.when(e == 0)
    def _():
        acc_ref[...] = jnp.zeros_like(acc_ref)

    tb = xbt_ref.shape[1]
    tok_tile = min(MXU_DIM, tb)
    n_tok_tiles = tb // tok_tile
    n_key_tiles = EXPERT_CHUNK // MXU_DIM
    rows_per_tile = MXU_DIM // N_KEYS

    def hidden(ht_out, n):
        cols = slice(n * tok_tile, (n + 1) * tok_tile)
        ht_out[:, cols] = jnp.dot(u_ref[...], xbt_ref[:, cols], preferred_element_type=F32)

    def gate(ht_in, r):
        w = None
        for h in range(PEER_HEADS):
            pair = s2_ref[h] + s1_ref[h, r:r + 1, :]
            term = jnp.where(pair >= tau_ref[h:h + 1, :], b_ref[h] * a_ref[h, r:r + 1, :], 0.0)
            w = term if w is None else w + term
        g = _gelu(ht_in[r * N_KEYS:(r + 1) * N_KEYS, :]) * w
        g_ref[r * N_KEYS:(r + 1) * N_KEYS, :] = g.astype(BF16)

    def step(ht_in, ht_out):
        tot = None
        half = n_key_tiles // 2
        for k in range(n_key_tiles):
            if k == half:
                pl.delay(1)
            if k % half == 0:
                for n in range(n_tok_tiles):
                    if n * 2 // n_tok_tiles == k // half:
                        hidden(ht_out, n)
            for r in range(k * rows_per_tile, (k + 1) * rows_per_tile):
                gate(ht_in, r)
            rows = slice(k * MXU_DIM, (k + 1) * MXU_DIM)
            part = jnp.dot(vt_ref[:, rows], g_ref[rows, :], preferred_element_type=F32)
            tot = part if tot is None else tot + part
        acc_ref[...] += tot

    @pl.when(lax.rem(s, 2) == 0)
    def _():
        step(ht1_ref, ht0_ref)

    @pl.when(lax.rem(s, 2) == 1)
    def _():
        step(ht0_ref, ht1_ref)

    @pl.when(jnp.logical_and(e == n_chunks - 1, s > 0))
    def _():
        o_ref[...] = acc_ref[...].T


def _peer_experts(xbt, u_bf, vt_bf, s1, a, s2, b, tau, tb):
    t = xbt.shape[1]
    n_chunks = N_EXPERTS // EXPERT_CHUNK
    n = (t // tb) * n_chunks
    lag = lambda s, d: jnp.clip(s - d, 0, n - 1)
    chunk = lambda s, d: lax.rem(lag(s, d), n_chunks)
    block = lambda s, d: lag(s, d) // n_chunks
    rows = pl.BlockSpec((PEER_HEADS, KEY_ROWS, tb), lambda s: (0, chunk(s, 1), block(s, 1)))
    full = pl.BlockSpec((PEER_HEADS, N_KEYS, tb), lambda s: (0, 0, block(s, 1)))
    return pl.pallas_call(
        functools.partial(_peer_experts_kernel, n_chunks=n_chunks),
        grid=(n + 1,),
        in_specs=[pl.BlockSpec((EXPERT_CHUNK, D_MODEL), lambda s: (chunk(s, 0), 0)),
                  pl.BlockSpec((D_MODEL, tb), lambda s: (0, block(s, 0))),
                  pl.BlockSpec((D_MODEL, EXPERT_CHUNK), lambda s: (0, chunk(s, 1))),
                  rows, rows, full, full,
                  pl.BlockSpec((PEER_HEADS, tb), lambda s: (0, block(s, 1)))],
        out_specs=pl.BlockSpec((tb, D_MODEL), lambda s: (block(s, 1), 0)),
        out_shape=jax.ShapeDtypeStruct((t, D_MODEL), F32),
        scratch_shapes=[pltpu.VMEM((D_MODEL, tb), F32), pltpu.VMEM((EXPERT_CHUNK, tb), F32),
                        pltpu.VMEM((EXPERT_CHUNK, tb), F32), pltpu.VMEM((EXPERT_CHUNK, tb), BF16)],
        compiler_params=_params("arbitrary"),
        name="peer_experts",
    )(u_bf, xbt, vt_bf, s1, a, s2, b, tau)


def _final_kernel(h_ref, y_ref, p_ref, gp_ref, wg_ref, wp_ref, gf_ref, o_ref):
    h = h_ref[...] + y_ref[...]
    ple = jnp.dot(p_ref[...].astype(BF16), wp_ref[...], preferred_element_type=F32)
    gate = jax.nn.sigmoid(jnp.dot(_rms(h, gp_ref[...]).astype(BF16), wg_ref[...],
                                  preferred_element_type=F32))
    o_ref[...] = _rms(h + ple * gate, gf_ref[...])


def _final(h, y, p, g_ple, wg_bf, wp_bf, g_final, tm):
    t = h.shape[0]
    row = lambda w: pl.BlockSpec((tm, w), lambda i: (i, 0))
    return pl.pallas_call(
        _final_kernel,
        grid=(t // tm,),
        in_specs=[row(D_MODEL), row(D_MODEL), row(PLE_DIM), _resident((1, D_MODEL)),
                  _resident((D_MODEL, D_MODEL)), _resident((PLE_DIM, D_MODEL)), _resident((1, D_MODEL))],
        out_specs=row(D_MODEL),
        out_shape=jax.ShapeDtypeStruct((t, D_MODEL), F32),
        compiler_params=_params("parallel"),
        name="final",
    )(h, y, p, g_ple.reshape(1, D_MODEL), wg_bf, wp_bf, g_final.reshape(1, D_MODEL))


def _tile(t, pref):
    return pref if t % pref == 0 else t


def _channel_mix(h1, p, w, tm, tb):
    q, xbt = _peer_query(h1, w["g_ffn"], w["wq"], tm)
    s1, a, s2, b, tau = _peer_select(q, w["keys"], tb)
    y = _peer_experts(xbt, w["u"], w["vt"], s1, a, s2, b, tau, tb)
    return _final(h1, y, p, w["g_ple"], w["wg"], w["wp"], w["g_final"], tm)


def kernel(x_prompt, x_sample, cache_k, cache_v, state_conv, p_prompt, p_sample, g_mix, w_in, conv_w,
           attn_sinks, g_attn_out, g_conv_out, w_out, g_ffn, w_peer_q, peer_sub_keys, expert_u,
           expert_v, g_ple, w_ple_gate, w_ple, g_final):
    seq = x_prompt.shape[1]
    nb = x_sample.shape[0]
    w = {
        "g_ffn": g_ffn[0], "wq": w_peer_q[0].astype(BF16), "keys": peer_sub_keys[0].astype(BF16),
        "u": expert_u[0].astype(BF16), "vt": expert_v[0].astype(BF16).T,
        "g_ple": g_ple[0], "wg": w_ple_gate[0].astype(BF16), "wp": w_ple[0].astype(BF16),
        "g_final": g_final,
    }
    win_bf = w_in[0].astype(BF16)
    wout_bf = w_out[0].astype(BF16)
    sinks = attn_sinks[0]

    xp = x_prompt[0]
    tm = _tile(seq, 256)
    tb = _tile(seq, 512)
    q, k, v, gb, u = _in_proj(xp, g_mix[0], win_bf, jnp.arange(seq, dtype=jnp.int32), tm)
    h1 = _mixer_prompt(q, k, v, gb, u, xp, wout_bf, conv_w[0], sinks, g_attn_out[0], g_conv_out[0])
    y_prompt = _channel_mix(h1, p_prompt[0, 0], w, tm, tb)[None]
    new_k_prompt = k[seq - WINDOW:].reshape(1, 1, WINDOW, N_KV_HEADS, HEAD_DIM)
    new_v_prompt = v[seq - WINDOW:].reshape(1, 1, WINDOW, N_KV_HEADS, HEAD_DIM)
    new_conv_prompt = u[seq - (CONV_K - 1):].reshape(1, 1, CONV_K - 1, CONV_WIDTH)

    xs = x_sample[:, 0]
    pos_s = jnp.full((nb,), PAST_LEN, jnp.int32)
    qs, ks, vs, gbs, us = _in_proj(xs, g_mix[0], win_bf, pos_s, nb)
    attn_s, nk, nv = _attn_sample(qs, ks, vs, cache_k[0].reshape(nb, WINDOW, KV_WIDTH),
                                  cache_v[0].reshape(nb, WINDOW, KV_WIDTH), sinks)
    h1s, ns = _mix_sample(attn_s, gbs, us, state_conv[0].reshape(nb, (CONV_K - 1) * CONV_WIDTH), xs,
                          wout_bf, conv_w[0], g_attn_out[0], g_conv_out[0])
    y_sample = _channel_mix(h1s, p_sample[0, :, 0], w, nb, nb)[:, None]
    new_k_sample = nk.reshape(1, nb, WINDOW, N_KV_HEADS, HEAD_DIM)
    new_v_sample = nv.reshape(1, nb, WINDOW, N_KV_HEADS, HEAD_DIM)
    new_conv_sample = ns.reshape(1, nb, CONV_K - 1, CONV_WIDTH)

    return (y_prompt, y_sample, new_k_prompt, new_v_prompt, new_conv_prompt,
            new_k_sample, new_v_sample, new_conv_sample)
```

```python
import functools

import jax
import jax.numpy as jnp
from jax import lax
from jax.experimental import pallas as pl
from jax.experimental.pallas import tpu as pltpu

F32 = jnp.float32
BF16 = jnp.bfloat16

D_MODEL = 2048
HEAD_DIM = 64
N_HEADS = 16
N_KV_HEADS = 4
GROUP = N_HEADS // N_KV_HEADS
ATTN_WIDTH = N_HEADS * HEAD_DIM
KV_WIDTH = N_KV_HEADS * HEAD_DIM
CONV_WIDTH = D_MODEL - ATTN_WIDTH
CONV_K = 3
WINDOW = 128
ATTN_SCALE = HEAD_DIM ** -0.5
ROPE_THETA = 500000.0
ROPE_DIM = HEAD_DIM // 4
ROPE_HALF = ROPE_DIM // 2
IN_COLS = ATTN_WIDTH + 2 * KV_WIDTH + 3 * CONV_WIDTH
PAST_LEN = 16384
PEER_HEADS = 8
N_KEYS = 128
N_EXPERTS = N_KEYS * N_KEYS
PEER_TOPK = 16
PEER_QDIM = 256
PEER_HALF = PEER_QDIM // 2
PLE_DIM = 256
EPS = 1e-6
NEG = -1e30
NEG_INF = float("-inf")

LANES = 128
SUBLANES = 8
MXU_DIM = 256
VMEM_LIMIT = 56 * 1024 * 1024


def _params(*sem, flags=None):
    return pltpu.CompilerParams(dimension_semantics=sem, vmem_limit_bytes=VMEM_LIMIT, flags=flags)


def _resident(shape):
    zeros = (0,) * len(shape)
    return pl.BlockSpec(shape, lambda *_: zeros, pipeline_mode=pl.Buffered(1))


def _rms(x, g):
    ms = jnp.mean(x * x, axis=-1, keepdims=True)
    return (x * lax.rsqrt(ms + EPS)) * g


def _in_proj_kernel(x_ref, g_ref, w_ref, c_ref, sa_ref, sb_ref,
                    q_ref, k_ref, v_ref, gb_ref, u_ref):
    a = _rms(x_ref[...], g_ref[...])
    z = jnp.dot(a.astype(BF16), w_ref[...], preferred_element_type=F32)
    o_k = ATTN_WIDTH
    o_v = o_k + KV_WIDTH
    o_gb = o_v + KV_WIDTH
    o_gc = o_gb + CONV_WIDTH
    o_xc = o_gc + CONV_WIDTH

    def rope(t):
        width = t.shape[1]
        reps = width // LANES
        c = jnp.concatenate([c_ref[...]] * reps, axis=1)
        sa = jnp.concatenate([sa_ref[...]] * reps, axis=1)
        sb = jnp.concatenate([sb_ref[...]] * reps, axis=1)
        up = pltpu.roll(t, width - ROPE_HALF, axis=1)
        dn = pltpu.roll(t, ROPE_HALF, axis=1)
        return t * c + up * sa + dn * sb

    q_ref[...] = rope(z[:, :o_k])
    k_ref[...] = rope(z[:, o_k:o_v])
    v_ref[...] = z[:, o_v:o_gb]
    gb_ref[...] = z[:, o_gb:o_gc]
    u_ref[...] = z[:, o_gc:o_xc] * z[:, o_xc:]


def _rope_tables(pos):
    inv = ROPE_THETA ** (-jnp.arange(ROPE_HALF, dtype=F32) / ROPE_HALF)
    lane = jnp.arange(LANES) % HEAD_DIM
    inv_lane = jnp.where(lane < ROPE_DIM, inv[lane % ROPE_HALF], 0.0)
    ang = pos.astype(F32)[:, None] * inv_lane[None, :]
    cos, sin = jnp.cos(ang), jnp.sin(ang)
    sa = jnp.where(lane < ROPE_HALF, -sin, 0.0)
    sb = jnp.where(lane >= ROPE_HALF, sin, 0.0)
    return cos, sa, sb


def _in_proj(x, g, w_bf, pos, tm):
    t = x.shape[0]
    c, sa, sb = _rope_tables(pos)
    row = lambda w: pl.BlockSpec((tm, w), lambda i: (i, 0))
    return pl.pallas_call(
        _in_proj_kernel,
        grid=(t // tm,),
        in_specs=[row(D_MODEL), _resident((1, D_MODEL)), _resident((D_MODEL, IN_COLS)),
                  row(LANES), row(LANES), row(LANES)],
        out_specs=[row(ATTN_WIDTH), row(KV_WIDTH), row(KV_WIDTH), row(CONV_WIDTH), row(CONV_WIDTH)],
        out_shape=[jax.ShapeDtypeStruct((t, ATTN_WIDTH), F32),
                   jax.ShapeDtypeStruct((t, KV_WIDTH), F32),
                   jax.ShapeDtypeStruct((t, KV_WIDTH), F32),
                   jax.ShapeDtypeStruct((t, CONV_WIDTH), F32),
                   jax.ShapeDtypeStruct((t, CONV_WIDTH), F32)],
        compiler_params=_params("parallel"),
        name="in_proj",
    )(x, g.reshape(1, D_MODEL), w_bf, c, sa, sb)


def _mix_tail(attn, conv_out, x, ga, gc, wout):
    merged = jnp.concatenate([_rms(attn, ga), _rms(conv_out, gc)], axis=1)
    return x + jnp.dot(merged.astype(BF16), wout, preferred_element_type=F32)


def _mixer_prompt_kernel(sinks_ref, q_ref, kp_ref, kc_ref, vp_ref, vc_ref, gb_ref, up_ref, uc_ref,
                         x_ref, wout_ref, convw_ref, ga_ref, gc_ref, h_ref):
    n = pl.program_id(0)
    q = q_ref[...]
    kband = jnp.concatenate([kp_ref[...], kc_ref[...]], axis=0).astype(BF16)
    vband = jnp.concatenate([vp_ref[...], vc_ref[...]], axis=0).astype(BF16)
    qi = lax.broadcasted_iota(jnp.int32, (WINDOW, 2 * WINDOW), 0) + WINDOW
    kj = lax.broadcasted_iota(jnp.int32, (WINDOW, 2 * WINDOW), 1)
    diff = qi - kj
    mask = jnp.where(diff >= 0, jnp.where(diff < WINDOW, jnp.where(kj + n * WINDOW >= WINDOW, 1, 0), 0), 0) > 0
    outs = []
    for h in range(N_HEADS):
        g = h // GROUP
        qh = q[:, h * HEAD_DIM:(h + 1) * HEAD_DIM].astype(BF16)
        kg = kband[:, g * HEAD_DIM:(g + 1) * HEAD_DIM]
        vg = vband[:, g * HEAD_DIM:(g + 1) * HEAD_DIM]
        s = lax.dot_general(qh, kg, (((1,), (1,)), ((), ())), preferred_element_type=F32) * ATTN_SCALE
        s = jnp.where(mask, s, NEG)
        sk = sinks_ref[h]
        m = jnp.maximum(jnp.max(s, axis=-1, keepdims=True), sk)
        e = jnp.where(mask, jnp.exp(s - m), 0.0)
        den = jnp.sum(e, axis=-1, keepdims=True) + jnp.exp(sk - m)
        p = e / den
        outs.append(jnp.dot(p.astype(BF16), vg, preferred_element_type=F32))
    attn = jnp.concatenate(outs, axis=1)

    u = uc_ref[...]
    prev = jnp.where(n > 0, up_ref[...], 0.0)
    ext = jnp.concatenate([prev, u], axis=0)
    cw = convw_ref[...]
    y = (ext[SUBLANES - 2:SUBLANES - 2 + WINDOW] * cw[0:1]
         + ext[SUBLANES - 1:SUBLANES - 1 + WINDOW] * cw[1:2]
         + u * cw[2:3])
    conv_out = gb_ref[...] * y
    h_ref[...] = _mix_tail(attn, conv_out, x_ref[...], ga_ref[...], gc_ref[...], wout_ref[...])


def _mixer_prompt(q, k, v, gb, u, x, wout_bf, conv_w, sinks, ga, gc):
    t = q.shape[0]
    nb = t // WINDOW
    cur = lambda w: pl.BlockSpec((WINDOW, w), lambda n: (n, 0))
    prv = lambda w: pl.BlockSpec((WINDOW, w), lambda n: (jnp.maximum(n - 1, 0), 0))
    rows_per_blk = WINDOW // SUBLANES
    tail = pl.BlockSpec((SUBLANES, CONV_WIDTH), lambda n: (jnp.maximum(n * rows_per_blk - 1, 0), 0))
    return pl.pallas_call(
        _mixer_prompt_kernel,
        grid=(nb,),
        in_specs=[pl.BlockSpec(memory_space=pltpu.SMEM),
                  cur(ATTN_WIDTH), prv(KV_WIDTH), cur(KV_WIDTH), prv(KV_WIDTH), cur(KV_WIDTH),
                  cur(CONV_WIDTH), tail, cur(CONV_WIDTH), cur(D_MODEL),
                  _resident((D_MODEL, D_MODEL)), _resident((SUBLANES, CONV_WIDTH)),
                  _resident((1, ATTN_WIDTH)), _resident((1, CONV_WIDTH))],
        out_specs=cur(D_MODEL),
        out_shape=jax.ShapeDtypeStruct((t, D_MODEL), F32),
        compiler_params=_params("parallel"),
        name="mixer_prompt",
    )(sinks, q, k, k, v, v, gb, u, u, x, wout_bf,
      jnp.pad(conv_w, ((0, SUBLANES - CONV_K), (0, 0))), ga.reshape(1, -1), gc.reshape(1, -1))


SAMPLE_ROWS = 8


def _attn_sample_kernel(q_ref, kn_ref, vn_ref, ck_ref, cv_ref, sk_ref, o_ref, nk_ref, nv_ref):
    hl = lax.broadcasted_iota(jnp.int32, (N_HEADS, KV_WIDTH), 1) // HEAD_DIM
    hg = lax.broadcasted_iota(jnp.int32, (N_HEADS, KV_WIDTH), 0) // GROUP
    own = hl == hg
    last = lax.broadcasted_iota(jnp.int32, (WINDOW, KV_WIDTH), 0) == WINDOW - 1
    sk = sk_ref[...][:, 0:1]
    kn = kn_ref[...]
    vn = vn_ref[...]
    for b in range(SAMPLE_ROWS):
        nk = jnp.where(last, kn[b:b + 1], pltpu.roll(ck_ref[b], WINDOW - 1, axis=0))
        nv = jnp.where(last, vn[b:b + 1], pltpu.roll(cv_ref[b], WINDOW - 1, axis=0))
        nk_ref[b] = nk
        nv_ref[b] = nv
        qb = q_ref[b]
        qw = jnp.where(own, jnp.concatenate([qb] * N_KV_HEADS, axis=1), 0.0)
        s = lax.dot_general(qw.astype(BF16), nk.astype(BF16), (((1,), (1,)), ((), ())),
                            preferred_element_type=F32) * ATTN_SCALE
        m = jnp.maximum(jnp.max(s, axis=-1, keepdims=True), sk)
        e = jnp.exp(s - m)
        p = e / (jnp.sum(e, axis=-1, keepdims=True) + jnp.exp(sk - m))
        pv = jnp.where(own, jnp.dot(p.astype(BF16), nv.astype(BF16), preferred_element_type=F32), 0.0)
        o = pv[:, 0:HEAD_DIM]
        for g in range(1, N_KV_HEADS):
            o = o + pv[:, g * HEAD_DIM:(g + 1) * HEAD_DIM]
        o_ref[b] = o


def _attn_sample(q, kn, vn, ck, cv, sinks):
    b = q.shape[0]
    q3 = q.reshape(b, N_HEADS, HEAD_DIM)
    sk = jnp.broadcast_to(sinks.reshape(N_HEADS, 1), (N_HEADS, LANES))
    r = SAMPLE_ROWS
    o3, nk, nv = pl.pallas_call(
        _attn_sample_kernel,
        grid=(b // r,),
        in_specs=[pl.BlockSpec((r, N_HEADS, HEAD_DIM), lambda i: (i, 0, 0)),
                  pl.BlockSpec((r, KV_WIDTH), lambda i: (i, 0)),
                  pl.BlockSpec((r, KV_WIDTH), lambda i: (i, 0)),
                  pl.BlockSpec((r, WINDOW, KV_WIDTH), lambda i: (i, 0, 0)),
                  pl.BlockSpec((r, WINDOW, KV_WIDTH), lambda i: (i, 0, 0)),
                  _resident((N_HEADS, LANES))],
        out_specs=[pl.BlockSpec((r, N_HEADS, HEAD_DIM), lambda i: (i, 0, 0)),
                   pl.BlockSpec((r, WINDOW, KV_WIDTH), lambda i: (i, 0, 0)),
                   pl.BlockSpec((r, WINDOW, KV_WIDTH), lambda i: (i, 0, 0))],
        out_shape=[jax.ShapeDtypeStruct((b, N_HEADS, HEAD_DIM), F32),
                   jax.ShapeDtypeStruct((b, WINDOW, KV_WIDTH), F32),
                   jax.ShapeDtypeStruct((b, WINDOW, KV_WIDTH), F32)],
        compiler_params=_params("parallel"),
        name="attn_sample",
    )(q3, kn, vn, ck, cv, sk)
    return o3.reshape(b, ATTN_WIDTH), nk, nv


def _mix_sample_kernel(attn_ref, gb_ref, u_ref, st_ref, x_ref, wout_ref, convw_ref, ga_ref, gc_ref,
                       h_ref, ns_ref):
    u = u_ref[...]
    st = st_ref[...]
    p0 = st[:, :CONV_WIDTH]
    p1 = st[:, CONV_WIDTH:]
    cw = convw_ref[...]
    y = p0 * cw[0:1] + p1 * cw[1:2] + u * cw[2:3]
    ns_ref[...] = jnp.concatenate([p1, u], axis=1)
    h_ref[...] = _mix_tail(attn_ref[...], gb_ref[...] * y, x_ref[...], ga_ref[...], gc_ref[...],
                           wout_ref[...])


def _mix_sample(attn, gb, u, state2, x, wout_bf, conv_w, ga, gc):
    b = attn.shape[0]
    full = lambda w: pl.BlockSpec((b, w), lambda i: (0, 0))
    return pl.pallas_call(
        _mix_sample_kernel,
        grid=(1,),
        in_specs=[full(ATTN_WIDTH), full(CONV_WIDTH), full(CONV_WIDTH), full(2 * CONV_WIDTH),
                  full(D_MODEL), _resident((D_MODEL, D_MODEL)), _resident((SUBLANES, CONV_WIDTH)),
                  _resident((1, ATTN_WIDTH)), _resident((1, CONV_WIDTH))],
        out_specs=[full(D_MODEL), full(2 * CONV_WIDTH)],
        out_shape=[jax.ShapeDtypeStruct((b, D_MODEL), F32),
                   jax.ShapeDtypeStruct((b, 2 * CONV_WIDTH), F32)],
        compiler_params=_params("arbitrary"),
        name="mix_sample",
    )(attn, gb, u, state2, x, wout_bf,
      jnp.pad(conv_w, ((0, SUBLANES - CONV_K), (0, 0))), ga.reshape(1, -1), gc.reshape(1, -1))


def _peer_query_kernel(h_ref, g_ref, w_ref, q_ref, xbt_ref):
    a = _rms(h_ref[...], g_ref[...])
    xbt_ref[...] = a.T.astype(BF16)
    q_ref[...] = jnp.dot(a.astype(BF16), w_ref[...], preferred_element_type=F32)


def _peer_query(h, g, wq_bf, tm):
    t = h.shape[0]
    row = lambda w: pl.BlockSpec((tm, w), lambda i: (i, 0))
    return pl.pallas_call(
        _peer_query_kernel,
        grid=(t // tm,),
        in_specs=[row(D_MODEL), _resident((1, D_MODEL)), _resident((D_MODEL, PEER_HEADS * PEER_QDIM))],
        out_specs=[row(PEER_HEADS * PEER_QDIM), pl.BlockSpec((D_MODEL, tm), lambda i: (0, i))],
        out_shape=[jax.ShapeDtypeStruct((t, PEER_HEADS * PEER_QDIM), F32),
                   jax.ShapeDtypeStruct((D_MODEL, t), BF16)],
        compiler_params=_params("parallel"),
        name="peer_query",
    )(h, g.reshape(1, D_MODEL), wq_bf)


def _top_sorted(s):
    rows = lax.broadcasted_iota(jnp.int32, (PEER_TOPK, s.shape[1]), 0)
    top = jnp.zeros((PEER_TOPK, s.shape[1]), F32)
    work = s
    for r in range(PEER_TOPK):
        m = jnp.max(work, axis=0, keepdims=True)
        top = jnp.where(rows == r, m, top)
        work = jnp.where(work >= m, NEG_INF, work)
    return top


def _peer_select_kernel(q_ref, keys_ref, thr_ref, a_ref, s2_ref, b_ref):
    tb = q_ref.shape[0]
    q = q_ref[...].astype(BF16)
    for h in range(PEER_HEADS):
        sc = []
        for c in range(2):
            col = (2 * h + c) * PEER_HALF
            sc.append(lax.dot_general(keys_ref[h, c], q[:, col:col + PEER_HALF],
                                      (((1,), (1,)), ((), ())), preferred_element_type=F32))
        s1, s2 = sc
        t1 = _top_sorted(s1)
        t2 = _top_sorted(s2)
        r16 = lax.broadcasted_iota(jnp.int32, (16, tb), 0)
        r8 = lax.broadcasted_iota(jnp.int32, (8, tb), 0)
        pieces = [t1[0:1] + t2]
        for r in range(1, 5):
            pieces.append(t1[r:r + 1] + t2[0:8])
        pieces.append(jnp.where(r16 >= 5, t1 + t2[0:1], NEG_INF))
        pieces.append(jnp.where(r8 >= 5, t1[0:8] + t2[1:2], NEG_INF))
        cand = jnp.concatenate(pieces, axis=0)
        work = cand
        for _ in range(PEER_TOPK):
            tau = jnp.max(work, axis=0, keepdims=True)
            work = jnp.where(work >= tau, NEG_INF, work)
        top = t1[0:1] + t2[0:1]
        z = jnp.sum(jnp.where(cand >= tau, jnp.exp(cand - top), 0.0), axis=0, keepdims=True)
        thr_rank = jnp.full((PEER_TOPK, tb), jnp.inf, F32)
        for c in range(PEER_TOPK):
            thr_rank = jnp.where(t1 + t2[c:c + 1] >= tau, t2[c:c + 1], thr_rank)
        thr = jnp.full((N_KEYS, tb), jnp.inf, F32)
        for r in range(PEER_TOPK):
            thr = jnp.where(s1 == t1[r:r + 1], thr_rank[r:r + 1], thr)
        thr_ref[h] = thr
        s2_ref[h] = s2
        a_ref[h] = jnp.exp(s1 - t1[0:1]) / z
        b_ref[h] = jnp.exp(s2 - t2[0:1])


def _peer_select(q, keys_bf, tb):
    t = q.shape[0]
    big = pl.BlockSpec((PEER_HEADS, N_KEYS, tb), lambda i: (0, 0, i))
    big_shape = jax.ShapeDtypeStruct((PEER_HEADS, N_KEYS, t), F32)
    return pl.pallas_call(
        _peer_select_kernel,
        grid=(t // tb,),
        in_specs=[pl.BlockSpec((tb, PEER_HEADS * PEER_QDIM), lambda i: (i, 0)),
                  _resident((PEER_HEADS, 2, N_KEYS, PEER_HALF))],
        out_specs=[big, big, big, big],
        out_shape=[big_shape, big_shape, big_shape, big_shape],
        compiler_params=_params("parallel"),
        name="peer_select",
    )(q, keys_bf)


EXPERT_CHUNK = 1024
KEY_ROWS = EXPERT_CHUNK // N_KEYS


def _gelu(x):
    return 0.5 * x * (1.0 + lax.erf(x * (2.0 ** -0.5)))


HALF_CHUNK = EXPERT_CHUNK // 2
HALF_ROWS = KEY_ROWS // 2


def _peer_experts_kernel(u_ref, xbt_ref, vt_ref, thr_ref, a_ref, s2_ref, b_ref,
                         o_ref, acc_ref, ht0_ref, ht1_ref, g_ref, *, n_chunks):
    s = pl.program_id(0)
    e = lax.rem(jnp.maximum(s - 1, 0), n_chunks)

    @pl.when(s == 0)
    def _():
        ht1_ref[...] = jnp.zeros_like(ht1_ref)

    @pl.when(e == 0)
    def _():
        acc_ref[...] = jnp.zeros_like(acc_ref)

    tb = xbt_ref.shape[1]
    tok_tile = min(MXU_DIM, tb)
    n_tok_tiles = tb // tok_tile
    n_key_tiles = EXPERT_CHUNK // MXU_DIM
    rows_per_tile = MXU_DIM // N_KEYS

    def hidden(ht_out, n):
        cols = slice(n * tok_tile, (n + 1) * tok_tile)
        ht_out[:, cols] = jnp.dot(u_ref[...], xbt_ref[:, cols], preferred_element_type=F32)

    def gate(ht_in, r):
        w = None
        for h in range(PEER_HEADS):
            term = jnp.where(s2_ref[h] >= thr_ref[h, r:r + 1, :], b_ref[h] * a_ref[h, r:r + 1, :], 0.0)
            w = term if w is None else w + term
        g = _gelu(ht_in[r * N_KEYS:(r + 1) * N_KEYS, :]) * w
        g_ref[r * N_KEYS:(r + 1) * N_KEYS, :] = g.astype(BF16)

    def step(ht_in, ht_out):
        tot = None
        half = n_key_tiles // 2
        for k in range(n_key_tiles):
            if k == half:
                pl.delay(1)
            if k % half == 0:
                for n in range(n_tok_tiles):
                    if n * 2 // n_tok_tiles == k // half:
                        hidden(ht_out, n)
            for r in range(k * rows_per_tile, (k + 1) * rows_per_tile):
                gate(ht_in, r)
            rows = slice(k * MXU_DIM, (k + 1) * MXU_DIM)
            part = jnp.dot(vt_ref[:, rows], g_ref[rows, :], preferred_element_type=F32)
            tot = part if tot is None else tot + part
        acc_ref[...] += tot

    @pl.when(lax.rem(s, 2) == 0)
    def _():
        step(ht1_ref, ht0_ref)

    @pl.when(lax.rem(s, 2) == 1)
    def _():
        step(ht0_ref, ht1_ref)

    @pl.when(jnp.logical_and(e == n_chunks - 1, s > 0))
    def _():
        o_ref[...] = acc_ref[...].T


def _peer_experts(xbt, u_bf, vt_bf, thr, a, s2, b, tb):
    t = xbt.shape[1]
    n_chunks = N_EXPERTS // EXPERT_CHUNK
    n = (t // tb) * n_chunks
    lag = lambda s, d: jnp.clip(s - d, 0, n - 1)
    chunk = lambda s, d: lax.rem(lag(s, d), n_chunks)
    block = lambda s, d: lag(s, d) // n_chunks
    rows = pl.BlockSpec((PEER_HEADS, KEY_ROWS, tb), lambda s: (0, chunk(s, 1), block(s, 1)))
    full = pl.BlockSpec((PEER_HEADS, N_KEYS, tb), lambda s: (0, 0, block(s, 1)))
    return pl.pallas_call(
        functools.partial(_peer_experts_kernel, n_chunks=n_chunks),
        grid=(n + 1,),
        in_specs=[pl.BlockSpec((EXPERT_CHUNK, D_MODEL), lambda s: (chunk(s, 0), 0)),
                  pl.BlockSpec((D_MODEL, tb), lambda s: (0, block(s, 0))),
                  pl.BlockSpec((D_MODEL, EXPERT_CHUNK), lambda s: (0, chunk(s, 1))),
                  rows, rows, full, full],
        out_specs=pl.BlockSpec((tb, D_MODEL), lambda s: (block(s, 1), 0)),
        out_shape=jax.ShapeDtypeStruct((t, D_MODEL), F32),
        scratch_shapes=[pltpu.VMEM((D_MODEL, tb), F32), pltpu.VMEM((EXPERT_CHUNK, tb), F32),
                        pltpu.VMEM((EXPERT_CHUNK, tb), F32), pltpu.VMEM((EXPERT_CHUNK, tb), BF16)],
        compiler_params=_params("arbitrary"),
        name="peer_experts",
    )(u_bf, xbt, vt_bf, thr, a, s2, b)


def _final_kernel(h_ref, y_ref, p_ref, gp_ref, wg_ref, wp_ref, gf_ref, o_ref):
    h = h_ref[...] + y_ref[...]
    ple = jnp.dot(p_ref[...].astype(BF16), wp_ref[...], preferred_element_type=F32)
    gate = jax.nn.sigmoid(jnp.dot(_rms(h, gp_ref[...]).astype(BF16), wg_ref[...],
                                  preferred_element_type=F32))
    o_ref[...] = _rms(h + ple * gate, gf_ref[...])


def _final(h, y, p, g_ple, wg_bf, wp_bf, g_final, tm):
    t = h.shape[0]
    row = lambda w: pl.BlockSpec((tm, w), lambda i: (i, 0))
    return pl.pallas_call(
        _final_kernel,
        grid=(t // tm,),
        in_specs=[row(D_MODEL), row(D_MODEL), row(PLE_DIM), _resident((1, D_MODEL)),
                  _resident((D_MODEL, D_MODEL)), _resident((PLE_DIM, D_MODEL)), _resident((1, D_MODEL))],
        out_specs=row(D_MODEL),
        out_shape=jax.ShapeDtypeStruct((t, D_MODEL), F32),
        compiler_params=_params("parallel"),
        name="final",
    )(h, y, p, g_ple.reshape(1, D_MODEL), wg_bf, wp_bf, g_final.reshape(1, D_MODEL))


def _tile(t, pref):
    return pref if t % pref == 0 else t


def _channel_mix(h1, p, w, tm, tb):
    q, xbt = _peer_query(h1, w["g_ffn"], w["wq"], tm)
    thr, a, s2, b = _peer_select(q, w["keys"], tb)
    y = _peer_experts(xbt, w["u"], w["vt"], thr, a, s2, b, tb)
    return _final(h1, y, p, w["g_ple"], w["wg"], w["wp"], w["g_final"], tm)


def kernel(x_prompt, x_sample, cache_k, cache_v, state_conv, p_prompt, p_sample, g_mix, w_in, conv_w,
           attn_sinks, g_attn_out, g_conv_out, w_out, g_ffn, w_peer_q, peer_sub_keys, expert_u,
           expert_v, g_ple, w_ple_gate, w_ple, g_final):
    seq = x_prompt.shape[1]
    nb = x_sample.shape[0]
    w = {
        "g_ffn": g_ffn[0], "wq": w_peer_q[0].astype(BF16), "keys": peer_sub_keys[0].astype(BF16),
        "u": expert_u[0].astype(BF16), "vt": expert_v[0].astype(BF16).T,
        "g_ple": g_ple[0], "wg": w_ple_gate[0].astype(BF16), "wp": w_ple[0].astype(BF16),
        "g_final": g_final,
    }
    win_bf = w_in[0].astype(BF16)
    wout_bf = w_out[0].astype(BF16)
    sinks = attn_sinks[0]

    xp = x_prompt[0]
    tm = _tile(seq, 256)
    tb = _tile(seq, 512)
    q, k, v, gb, u = _in_proj(xp, g_mix[0], win_bf, jnp.arange(seq, dtype=jnp.int32), tm)
    h1 = _mixer_prompt(q, k, v, gb, u, xp, wout_bf, conv_w[0], sinks, g_attn_out[0], g_conv_out[0])
    y_prompt = _channel_mix(h1, p_prompt[0, 0], w, tm, tb)[None]
    new_k_prompt = k[seq - WINDOW:].reshape(1, 1, WINDOW, N_KV_HEADS, HEAD_DIM)
    new_v_prompt = v[seq - WINDOW:].reshape(1, 1, WINDOW, N_KV_HEADS, HEAD_DIM)
    new_conv_prompt = u[seq - (CONV_K - 1):].reshape(1, 1, CONV_K - 1, CONV_WIDTH)

    xs = x_sample[:, 0]
    pos_s = jnp.full((nb,), PAST_LEN, jnp.int32)
    qs, ks, vs, gbs, us = _in_proj(xs, g_mix[0], win_bf, pos_s, nb)
    attn_s, nk, nv = _attn_sample(qs, ks, vs, cache_k[0].reshape(nb, WINDOW, KV_WIDTH),
                                  cache_v[0].reshape(nb, WINDOW, KV_WIDTH), sinks)
    h1s, ns = _mix_sample(attn_s, gbs, us, state_conv[0].reshape(nb, (CONV_K - 1) * CONV_WIDTH), xs,
                          wout_bf, conv_w[0], g_attn_out[0], g_conv_out[0])
    y_sample = _channel_mix(h1s, p_sample[0, :, 0], w, nb, nb)[:, None]
    new_k_sample = nk.reshape(1, nb, WINDOW, N_KV_HEADS, HEAD_DIM)
    new_v_sample = nv.reshape(1, nb, WINDOW, N_KV_HEADS, HEAD_DIM)
    new_conv_sample = ns.reshape(1, nb, CONV_K - 1, CONV_WIDTH)

    return (y_prompt, y_sample, new_k_prompt, new_v_prompt, new_conv_prompt,
            new_k_sample, new_v_sample, new_conv_sample)
```

```python
import functools

import jax
import jax.numpy as jnp
from jax import lax
from jax.experimental import pallas as pl
from jax.experimental.pallas import tpu as pltpu

F32 = jnp.float32
BF16 = jnp.bfloat16

D_MODEL = 2048
HEAD_DIM = 64
N_HEADS = 16
N_KV_HEADS = 4
GROUP = N_HEADS // N_KV_HEADS
ATTN_WIDTH = N_HEADS * HEAD_DIM
KV_WIDTH = N_KV_HEADS * HEAD_DIM
CONV_WIDTH = D_MODEL - ATTN_WIDTH
CONV_K = 3
WINDOW = 128
ATTN_SCALE = HEAD_DIM ** -0.5
ROPE_THETA = 500000.0
ROPE_DIM = HEAD_DIM // 4
ROPE_HALF = ROPE_DIM // 2
IN_COLS = ATTN_WIDTH + 2 * KV_WIDTH + 3 * CONV_WIDTH
PAST_LEN = 16384
PEER_HEADS = 8
N_KEYS = 128
N_EXPERTS = N_KEYS * N_KEYS
PEER_TOPK = 16
PEER_QDIM = 256
PEER_HALF = PEER_QDIM // 2
PLE_DIM = 256
EPS = 1e-6
NEG = -1e30
NEG_INF = float("-inf")

LANES = 128
SUBLANES = 8
MXU_DIM = 256
VMEM_LIMIT = 56 * 1024 * 1024


def _params(*sem, flags=None):
    return pltpu.CompilerParams(dimension_semantics=sem, vmem_limit_bytes=VMEM_LIMIT, flags=flags)


def _resident(shape):
    zeros = (0,) * len(shape)
    return pl.BlockSpec(shape, lambda *_: zeros, pipeline_mode=pl.Buffered(1))


def _rms(x, g):
    ms = jnp.mean(x * x, axis=-1, keepdims=True)
    return (x * lax.rsqrt(ms + EPS)) * g


def _in_proj_kernel(x_ref, g_ref, w_ref, c_ref, sa_ref, sb_ref,
                    q_ref, k_ref, v_ref, gb_ref, u_ref):
    a = _rms(x_ref[...], g_ref[...])
    z = jnp.dot(a.astype(BF16), w_ref[...], preferred_element_type=F32)
    o_k = ATTN_WIDTH
    o_v = o_k + KV_WIDTH
    o_gb = o_v + KV_WIDTH
    o_gc = o_gb + CONV_WIDTH
    o_xc = o_gc + CONV_WIDTH

    def rope(t):
        width = t.shape[1]
        reps = width // LANES
        c = jnp.concatenate([c_ref[...]] * reps, axis=1)
        sa = jnp.concatenate([sa_ref[...]] * reps, axis=1)
        sb = jnp.concatenate([sb_ref[...]] * reps, axis=1)
        up = pltpu.roll(t, width - ROPE_HALF, axis=1)
        dn = pltpu.roll(t, ROPE_HALF, axis=1)
        return t * c + up * sa + dn * sb

    q_ref[...] = rope(z[:, :o_k])
    k_ref[...] = rope(z[:, o_k:o_v])
    v_ref[...] = z[:, o_v:o_gb]
    gb_ref[...] = z[:, o_gb:o_gc]
    u_ref[...] = z[:, o_gc:o_xc] * z[:, o_xc:]


def _rope_tables(pos):
    inv = ROPE_THETA ** (-jnp.arange(ROPE_HALF, dtype=F32) / ROPE_HALF)
    lane = jnp.arange(LANES) % HEAD_DIM
    inv_lane = jnp.where(lane < ROPE_DIM, inv[lane % ROPE_HALF], 0.0)
    ang = pos.astype(F32)[:, None] * inv_lane[None, :]
    cos, sin = jnp.cos(ang), jnp.sin(ang)
    sa = jnp.where(lane < ROPE_HALF, -sin, 0.0)
    sb = jnp.where(lane >= ROPE_HALF, sin, 0.0)
    return cos, sa, sb


def _in_proj(x, g, w_bf, pos, tm):
    t = x.shape[0]
    c, sa, sb = _rope_tables(pos)
    row = lambda w: pl.BlockSpec((tm, w), lambda i: (i, 0))
    return pl.pallas_call(
        _in_proj_kernel,
        grid=(t // tm,),
        in_specs=[row(D_MODEL), _resident((1, D_MODEL)), _resident((D_MODEL, IN_COLS)),
                  row(LANES), row(LANES), row(LANES)],
        out_specs=[row(ATTN_WIDTH), row(KV_WIDTH), row(KV_WIDTH), row(CONV_WIDTH), row(CONV_WIDTH)],
        out_shape=[jax.ShapeDtypeStruct((t, ATTN_WIDTH), F32),
                   jax.ShapeDtypeStruct((t, KV_WIDTH), F32),
                   jax.ShapeDtypeStruct((t, KV_WIDTH), F32),
                   jax.ShapeDtypeStruct((t, CONV_WIDTH), F32),
                   jax.ShapeDtypeStruct((t, CONV_WIDTH), F32)],
        compiler_params=_params("parallel"),
        name="in_proj",
    )(x, g.reshape(1, D_MODEL), w_bf, c, sa, sb)


def _mix_tail(attn, conv_out, x, ga, gc, wout):
    merged = jnp.concatenate([_rms(attn, ga), _rms(conv_out, gc)], axis=1)
    return x + jnp.dot(merged.astype(BF16), wout, preferred_element_type=F32)


def _mixer_prompt_kernel(sinks_ref, q_ref, kp_ref, kc_ref, vp_ref, vc_ref, gb_ref, up_ref, uc_ref,
                         x_ref, wout_ref, convw_ref, ga_ref, gc_ref, h_ref):
    n = pl.program_id(0)
    q = q_ref[...]
    kband = jnp.concatenate([kp_ref[...], kc_ref[...]], axis=0).astype(BF16)
    vband = jnp.concatenate([vp_ref[...], vc_ref[...]], axis=0).astype(BF16)
    qi = lax.broadcasted_iota(jnp.int32, (WINDOW, 2 * WINDOW), 0) + WINDOW
    kj = lax.broadcasted_iota(jnp.int32, (WINDOW, 2 * WINDOW), 1)
    diff = qi - kj
    mask = jnp.where(diff >= 0, jnp.where(diff < WINDOW, jnp.where(kj + n * WINDOW >= WINDOW, 1, 0), 0), 0) > 0
    outs = []
    for h in range(N_HEADS):
        g = h // GROUP
        qh = q[:, h * HEAD_DIM:(h + 1) * HEAD_DIM].astype(BF16)
        kg = kband[:, g * HEAD_DIM:(g + 1) * HEAD_DIM]
        vg = vband[:, g * HEAD_DIM:(g + 1) * HEAD_DIM]
        s = lax.dot_general(qh, kg, (((1,), (1,)), ((), ())), preferred_element_type=F32) * ATTN_SCALE
        s = jnp.where(mask, s, NEG)
        sk = sinks_ref[h]
        m = jnp.maximum(jnp.max(s, axis=-1, keepdims=True), sk)
        e = jnp.where(mask, jnp.exp(s - m), 0.0)
        den = jnp.sum(e, axis=-1, keepdims=True) + jnp.exp(sk - m)
        p = e / den
        outs.append(jnp.dot(p.astype(BF16), vg, preferred_element_type=F32))
    attn = jnp.concatenate(outs, axis=1)

    u = uc_ref[...]
    prev = jnp.where(n > 0, up_ref[...], 0.0)
    ext = jnp.concatenate([prev, u], axis=0)
    cw = convw_ref[...]
    y = (ext[SUBLANES - 2:SUBLANES - 2 + WINDOW] * cw[0:1]
         + ext[SUBLANES - 1:SUBLANES - 1 + WINDOW] * cw[1:2]
         + u * cw[2:3])
    conv_out = gb_ref[...] * y
    h_ref[...] = _mix_tail(attn, conv_out, x_ref[...], ga_ref[...], gc_ref[...], wout_ref[...])


def _mixer_prompt(q, k, v, gb, u, x, wout_bf, conv_w, sinks, ga, gc):
    t = q.shape[0]
    nb = t // WINDOW
    cur = lambda w: pl.BlockSpec((WINDOW, w), lambda n: (n, 0))
    prv = lambda w: pl.BlockSpec((WINDOW, w), lambda n: (jnp.maximum(n - 1, 0), 0))
    rows_per_blk = WINDOW // SUBLANES
    tail = pl.BlockSpec((SUBLANES, CONV_WIDTH), lambda n: (jnp.maximum(n * rows_per_blk - 1, 0), 0))
    return pl.pallas_call(
        _mixer_prompt_kernel,
        grid=(nb,),
        in_specs=[pl.BlockSpec(memory_space=pltpu.SMEM),
                  cur(ATTN_WIDTH), prv(KV_WIDTH), cur(KV_WIDTH), prv(KV_WIDTH), cur(KV_WIDTH),
                  cur(CONV_WIDTH), tail, cur(CONV_WIDTH), cur(D_MODEL),
                  _resident((D_MODEL, D_MODEL)), _resident((SUBLANES, CONV_WIDTH)),
                  _resident((1, ATTN_WIDTH)), _resident((1, CONV_WIDTH))],
        out_specs=cur(D_MODEL),
        out_shape=jax.ShapeDtypeStruct((t, D_MODEL), F32),
        compiler_params=_params("parallel"),
        name="mixer_prompt",
    )(sinks, q, k, k, v, v, gb, u, u, x, wout_bf,
      jnp.pad(conv_w, ((0, SUBLANES - CONV_K), (0, 0))), ga.reshape(1, -1), gc.reshape(1, -1))


SAMPLE_ROWS = 8


def _attn_sample_kernel(q_ref, kn_ref, vn_ref, ck_ref, cv_ref, sk_ref, o_ref, nk_ref, nv_ref):
    hl = lax.broadcasted_iota(jnp.int32, (N_HEADS, KV_WIDTH), 1) // HEAD_DIM
    hg = lax.broadcasted_iota(jnp.int32, (N_HEADS, KV_WIDTH), 0) // GROUP
    own = hl == hg
    last = lax.broadcasted_iota(jnp.int32, (WINDOW, KV_WIDTH), 0) == WINDOW - 1
    sk = sk_ref[...][:, 0:1]
    kn = kn_ref[...]
    vn = vn_ref[...]
    for b in range(SAMPLE_ROWS):
        nk = jnp.where(last, kn[b:b + 1], pltpu.roll(ck_ref[b], WINDOW - 1, axis=0))
        nv = jnp.where(last, vn[b:b + 1], pltpu.roll(cv_ref[b], WINDOW - 1, axis=0))
        nk_ref[b] = nk
        nv_ref[b] = nv
        qb = q_ref[b]
        qw = jnp.where(own, jnp.concatenate([qb] * N_KV_HEADS, axis=1), 0.0)
        s = lax.dot_general(qw.astype(BF16), nk.astype(BF16), (((1,), (1,)), ((), ())),
                            preferred_element_type=F32) * ATTN_SCALE
        m = jnp.maximum(jnp.max(s, axis=-1, keepdims=True), sk)
        e = jnp.exp(s - m)
        p = e / (jnp.sum(e, axis=-1, keepdims=True) + jnp.exp(sk - m))
        pv = jnp.where(own, jnp.dot(p.astype(BF16), nv.astype(BF16), preferred_element_type=F32), 0.0)
        o = pv[:, 0:HEAD_DIM]
        for g in range(1, N_KV_HEADS):
            o = o + pv[:, g * HEAD_DIM:(g + 1) * HEAD_DIM]
        o_ref[b] = o


def _attn_sample(q, kn, vn, ck, cv, sinks):
    b = q.shape[0]
    q3 = q.reshape(b, N_HEADS, HEAD_DIM)
    sk = jnp.broadcast_to(sinks.reshape(N_HEADS, 1), (N_HEADS, LANES))
    r = SAMPLE_ROWS
    o3, nk, nv = pl.pallas_call(
        _attn_sample_kernel,
        grid=(b // r,),
        in_specs=[pl.BlockSpec((r, N_HEADS, HEAD_DIM), lambda i: (i, 0, 0)),
                  pl.BlockSpec((r, KV_WIDTH), lambda i: (i, 0)),
                  pl.BlockSpec((r, KV_WIDTH), lambda i: (i, 0)),
                  pl.BlockSpec((r, WINDOW, KV_WIDTH), lambda i: (i, 0, 0)),
                  pl.BlockSpec((r, WINDOW, KV_WIDTH), lambda i: (i, 0, 0)),
                  _resident((N_HEADS, LANES))],
        out_specs=[pl.BlockSpec((r, N_HEADS, HEAD_DIM), lambda i: (i, 0, 0)),
                   pl.BlockSpec((r, WINDOW, KV_WIDTH), lambda i: (i, 0, 0)),
                   pl.BlockSpec((r, WINDOW, KV_WIDTH), lambda i: (i, 0, 0))],
        out_shape=[jax.ShapeDtypeStruct((b, N_HEADS, HEAD_DIM), F32),
                   jax.ShapeDtypeStruct((b, WINDOW, KV_WIDTH), F32),
                   jax.ShapeDtypeStruct((b, WINDOW, KV_WIDTH), F32)],
        compiler_params=_params("parallel"),
        name="attn_sample",
    )(q3, kn, vn, ck, cv, sk)
    return o3.reshape(b, ATTN_WIDTH), nk, nv


def _mix_sample_kernel(attn_ref, gb_ref, u_ref, st_ref, x_ref, wout_ref, convw_ref, ga_ref, gc_ref,
                       h_ref, ns_ref):
    u = u_ref[...]
    st = st_ref[...]
    p0 = st[:, :CONV_WIDTH]
    p1 = st[:, CONV_WIDTH:]
    cw = convw_ref[...]
    y = p0 * cw[0:1] + p1 * cw[1:2] + u * cw[2:3]
    ns_ref[...] = jnp.concatenate([p1, u], axis=1)
    h_ref[...] = _mix_tail(attn_ref[...], gb_ref[...] * y, x_ref[...], ga_ref[...], gc_ref[...],
                           wout_ref[...])


def _mix_sample(attn, gb, u, state2, x, wout_bf, conv_w, ga, gc):
    b = attn.shape[0]
    full = lambda w: pl.BlockSpec((b, w), lambda i: (0, 0))
    return pl.pallas_call(
        _mix_sample_kernel,
        grid=(1,),
        in_specs=[full(ATTN_WIDTH), full(CONV_WIDTH), full(CONV_WIDTH), full(2 * CONV_WIDTH),
                  full(D_MODEL), _resident((D_MODEL, D_MODEL)), _resident((SUBLANES, CONV_WIDTH)),
                  _resident((1, ATTN_WIDTH)), _resident((1, CONV_WIDTH))],
        out_specs=[full(D_MODEL), full(2 * CONV_WIDTH)],
        out_shape=[jax.ShapeDtypeStruct((b, D_MODEL), F32),
                   jax.ShapeDtypeStruct((b, 2 * CONV_WIDTH), F32)],
        compiler_params=_params("arbitrary"),
        name="mix_sample",
    )(attn, gb, u, state2, x, wout_bf,
      jnp.pad(conv_w, ((0, SUBLANES - CONV_K), (0, 0))), ga.reshape(1, -1), gc.reshape(1, -1))


def _peer_query_kernel(h_ref, g_ref, w_ref, q_ref, xbt_ref):
    a = _rms(h_ref[...], g_ref[...])
    xbt_ref[...] = a.T.astype(BF16)
    q_ref[...] = jnp.dot(a.astype(BF16), w_ref[...], preferred_element_type=F32)


def _peer_query(h, g, wq_bf, tm):
    t = h.shape[0]
    row = lambda w: pl.BlockSpec((tm, w), lambda i: (i, 0))
    return pl.pallas_call(
        _peer_query_kernel,
        grid=(t // tm,),
        in_specs=[row(D_MODEL), _resident((1, D_MODEL)), _resident((D_MODEL, PEER_HEADS * PEER_QDIM))],
        out_specs=[row(PEER_HEADS * PEER_QDIM), pl.BlockSpec((D_MODEL, tm), lambda i: (0, i))],
        out_shape=[jax.ShapeDtypeStruct((t, PEER_HEADS * PEER_QDIM), F32),
                   jax.ShapeDtypeStruct((D_MODEL, t), BF16)],
        compiler_params=_params("parallel"),
        name="peer_query",
    )(h, g.reshape(1, D_MODEL), wq_bf)


def _sort16_pairs():
    def merge(lo, hi, r):
        step = r * 2
        if step < hi - lo:
            yield from merge(lo, hi, step)
            yield from merge(lo + r, hi, step)
            yield from [(i, i + r) for i in range(lo + r, hi - r, step)]
        else:
            yield (lo, lo + r)

    def sort(lo, hi):
        if hi - lo >= 1:
            mid = lo + (hi - lo) // 2
            yield from sort(lo, mid)
            yield from sort(mid + 1, hi)
            yield from merge(lo, hi, 1)

    return list(sort(0, PEER_TOPK - 1))


def _top_sorted(s, with_code=False):
    n_v = N_KEYS // SUBLANES
    assert n_v == PEER_TOPK
    tops, codes = [], []
    for c in range(s.shape[1] // LANES):
        cols = slice(c * LANES, (c + 1) * LANES)
        x = [s[i * SUBLANES:(i + 1) * SUBLANES, cols] for i in range(n_v)]
        v = list(x)

        def exchange(i, j):
            v[i], v[j] = jnp.maximum(v[i], v[j]), jnp.minimum(v[i], v[j])

        for i, j in _sort16_pairs():
            exchange(i, j)
        shift = SUBLANES // 2
        while shift >= 1:
            rolled = [pltpu.roll(t, shift, axis=0) for t in v]
            v = [jnp.maximum(v[i], rolled[n_v - 1 - i]) for i in range(n_v)]
            d = n_v // 2
            while d >= 1:
                for i in range(n_v):
                    if not i & d:
                        exchange(i, i + d)
                d //= 2
            shift //= 2
        tops.append(jnp.concatenate([t[0:1] for t in v], axis=0))
        if with_code:
            parts = []
            for xi in x:
                cnt = jnp.where(xi >= v[0], 1.0, 0.0)
                for t in v[1:]:
                    cnt = cnt + jnp.where(xi >= t, 1.0, 0.0)
                parts.append(cnt)
            codes.append(jnp.concatenate(parts, axis=0))
    top = jnp.concatenate(tops, axis=1)
    return top, (jnp.concatenate(codes, axis=1) if with_code else None)


def _peer_select_kernel(q_ref, keys_ref, thr_ref, a_ref, code_ref, b_ref):
    tb = q_ref.shape[0]
    q = q_ref[...].astype(BF16)
    for h in range(PEER_HEADS):
        sc = []
        for c in range(2):
            col = (2 * h + c) * PEER_HALF
            sc.append(lax.dot_general(keys_ref[h, c], q[:, col:col + PEER_HALF],
                                      (((1,), (1,)), ((), ())), preferred_element_type=F32))
        s1, s2 = sc
        t1, _ = _top_sorted(s1)
        t2, code2 = _top_sorted(s2, with_code=True)
        r16 = lax.broadcasted_iota(jnp.int32, (16, tb), 0)
        r8 = lax.broadcasted_iota(jnp.int32, (8, tb), 0)
        pieces = [t1[0:1] + t2]
        for r in range(1, 5):
            pieces.append(t1[r:r + 1] + t2[0:8])
        pieces.append(jnp.where(r16 >= 5, t1 + t2[0:1], NEG_INF))
        pieces.append(jnp.where(r8 >= 5, t1[0:8] + t2[1:2], NEG_INF))
        cand = jnp.concatenate(pieces, axis=0)
        work = cand
        for _ in range(PEER_TOPK):
            tau = jnp.max(work, axis=0, keepdims=True)
            work = jnp.where(work >= tau, NEG_INF, work)
        top = t1[0:1] + t2[0:1]
        z = jnp.sum(jnp.where(cand >= tau, jnp.exp(cand - top), 0.0), axis=0, keepdims=True)
        thr_rank = jnp.full((PEER_TOPK, tb), float(PEER_TOPK + 1), F32)
        for c in range(PEER_TOPK):
            thr_rank = thr_rank - jnp.where(t1 + t2[c:c + 1] >= tau, 1.0, 0.0)
        thr = jnp.full((N_KEYS, tb), float(PEER_TOPK + 1), F32)
        for r in range(PEER_TOPK):
            thr = jnp.where(s1 == t1[r:r + 1], thr_rank[r:r + 1], thr)
        thr_ref[h] = thr
        a_ref[h] = jnp.exp(s1 - t1[0:1]) / z
        code_ref[h] = code2.astype(BF16)
        b_ref[h] = jnp.exp(s2 - t2[0:1]).astype(BF16)


def _peer_select(q, keys_bf, tb):
    t = q.shape[0]
    big = pl.BlockSpec((PEER_HEADS, N_KEYS, tb), lambda i: (0, 0, i))
    big_shape = jax.ShapeDtypeStruct((PEER_HEADS, N_KEYS, t), F32)
    return pl.pallas_call(
        _peer_select_kernel,
        grid=(t // tb,),
        in_specs=[pl.BlockSpec((tb, PEER_HEADS * PEER_QDIM), lambda i: (i, 0)),
                  _resident((PEER_HEADS, 2, N_KEYS, PEER_HALF))],
        out_specs=[big, big, big, big],
        out_shape=[big_shape, big_shape, jax.ShapeDtypeStruct((PEER_HEADS, N_KEYS, t), BF16),
                   jax.ShapeDtypeStruct((PEER_HEADS, N_KEYS, t), BF16)],
        compiler_params=_params("parallel"),
        name="peer_select",
    )(q, keys_bf)


EXPERT_CHUNK = 1024
KEY_ROWS = EXPERT_CHUNK // N_KEYS


def _gelu(x):
    return 0.5 * x * (1.0 + lax.erf(x * (2.0 ** -0.5)))


HALF_CHUNK = EXPERT_CHUNK // 2
HALF_ROWS = KEY_ROWS // 2


def _peer_experts_kernel(u_ref, xbt_ref, vt_ref, thr_ref, a_ref, code_ref, b_ref,
                         o_ref, acc_ref, ht0_ref, ht1_ref, g_ref, *, n_chunks):
    s = pl.program_id(0)
    e = lax.rem(jnp.maximum(s - 1, 0), n_chunks)

    @pl.when(s == 0)
    def _():
        ht1_ref[...] = jnp.zeros_like(ht1_ref)

    @pl.when(e == 0)
    def _():
        acc_ref[...] = jnp.zeros_like(acc_ref)

    tb = xbt_ref.shape[1]
    tok_tile = min(MXU_DIM, tb)
    n_tok_tiles = tb // tok_tile

    def hidden(ht_out, n):
        cols = slice(n * tok_tile, (n + 1) * tok_tile)
        ht_out[:, cols] = jnp.dot(u_ref[...], xbt_ref[:, cols], preferred_element_type=F32)

    def spread(row):
        return jnp.broadcast_to(row.astype(BF16), (N_KEYS, tb))

    def gate(ht_in, r):
        w = None
        for h in range(PEER_HEADS):
            term = jnp.where(code_ref[h] >= spread(thr_ref[h, r:r + 1, :]),
                             b_ref[h] * spread(a_ref[h, r:r + 1, :]), jnp.zeros((), BF16))
            w = term if w is None else w + term
        rows = slice(r * N_KEYS, (r + 1) * N_KEYS)
        g_ref[rows, :] = _gelu(ht_in[rows, :]).astype(BF16) * w

    def step(ht_in, ht_out):
        for p in range(2):
            if p == 1:
                pl.delay(1)
            for n in range(n_tok_tiles):
                if n * 2 // n_tok_tiles == p:
                    hidden(ht_out, n)
            for r in range(p * HALF_ROWS, (p + 1) * HALF_ROWS):
                gate(ht_in, r)
            rows = slice(p * HALF_CHUNK, (p + 1) * HALF_CHUNK)
            acc_ref[...] += jnp.dot(vt_ref[:, rows], g_ref[rows, :], preferred_element_type=F32)

    @pl.when(lax.rem(s, 2) == 0)
    def _():
        step(ht1_ref, ht0_ref)

    @pl.when(lax.rem(s, 2) == 1)
    def _():
        step(ht0_ref, ht1_ref)

    @pl.when(jnp.logical_and(e == n_chunks - 1, s > 0))
    def _():
        o_ref[...] = acc_ref[...].T


def _peer_experts(xbt, u_bf, vt_bf, thr, a, s2, b, tb):
    t = xbt.shape[1]
    n_chunks = N_EXPERTS // EXPERT_CHUNK
    n = (t // tb) * n_chunks
    lag = lambda s, d: jnp.clip(s - d, 0, n - 1)
    chunk = lambda s, d: lax.rem(lag(s, d), n_chunks)
    block = lambda s, d: lag(s, d) // n_chunks
    rows = pl.BlockSpec((PEER_HEADS, KEY_ROWS, tb), lambda s: (0, chunk(s, 1), block(s, 1)))
    full = pl.BlockSpec((PEER_HEADS, N_KEYS, tb), lambda s: (0, 0, block(s, 1)))
    return pl.pallas_call(
        functools.partial(_peer_experts_kernel, n_chunks=n_chunks),
        grid=(n + 1,),
        in_specs=[pl.BlockSpec((EXPERT_CHUNK, D_MODEL), lambda s: (chunk(s, 0), 0)),
                  pl.BlockSpec((D_MODEL, tb), lambda s: (0, block(s, 0))),
                  pl.BlockSpec((D_MODEL, EXPERT_CHUNK), lambda s: (0, chunk(s, 1))),
                  rows, rows, full, full],
        out_specs=pl.BlockSpec((tb, D_MODEL), lambda s: (block(s, 1), 0)),
        out_shape=jax.ShapeDtypeStruct((t, D_MODEL), F32),
        scratch_shapes=[pltpu.VMEM((D_MODEL, tb), F32), pltpu.VMEM((EXPERT_CHUNK, tb), F32),
                        pltpu.VMEM((EXPERT_CHUNK, tb), F32), pltpu.VMEM((EXPERT_CHUNK, tb), BF16)],
        compiler_params=_params("arbitrary"),
        name="peer_experts",
    )(u_bf, xbt, vt_bf, thr, a, s2, b)


def _final_kernel(h_ref, y_ref, p_ref, gp_ref, wg_ref, wp_ref, gf_ref, o_ref):
    h = h_ref[...] + y_ref[...]
    ple = jnp.dot(p_ref[...].astype(BF16), wp_ref[...], preferred_element_type=F32)
    gate = jax.nn.sigmoid(jnp.dot(_rms(h, gp_ref[...]).astype(BF16), wg_ref[...],
                                  preferred_element_type=F32))
    o_ref[...] = _rms(h + ple * gate, gf_ref[...])


def _final(h, y, p, g_ple, wg_bf, wp_bf, g_final, tm):
    t = h.shape[0]
    row = lambda w: pl.BlockSpec((tm, w), lambda i: (i, 0))
    return pl.pallas_call(
        _final_kernel,
        grid=(t // tm,),
        in_specs=[row(D_MODEL), row(D_MODEL), row(PLE_DIM), _resident((1, D_MODEL)),
                  _resident((D_MODEL, D_MODEL)), _resident((PLE_DIM, D_MODEL)), _resident((1, D_MODEL))],
        out_specs=row(D_MODEL),
        out_shape=jax.ShapeDtypeStruct((t, D_MODEL), F32),
        compiler_params=_params("parallel"),
        name="final",
    )(h, y, p, g_ple.reshape(1, D_MODEL), wg_bf, wp_bf, g_final.reshape(1, D_MODEL))


def _tile(t, pref):
    return pref if t % pref == 0 else t


def _channel_mix(h1, p, w, tm, tb):
    q, xbt = _peer_query(h1, w["g_ffn"], w["wq"], tm)
    thr, a, s2, b = _peer_select(q, w["keys"], tb)
    y = _peer_experts(xbt, w["u"], w["vt"], thr, a, s2, b, tb)
    return _final(h1, y, p, w["g_ple"], w["wg"], w["wp"], w["g_final"], tm)


def kernel(x_prompt, x_sample, cache_k, cache_v, state_conv, p_prompt, p_sample, g_mix, w_in, conv_w,
           attn_sinks, g_attn_out, g_conv_out, w_out, g_ffn, w_peer_q, peer_sub_keys, expert_u,
           expert_v, g_ple, w_ple_gate, w_ple, g_final):
    seq = x_prompt.shape[1]
    nb = x_sample.shape[0]
    w = {
        "g_ffn": g_ffn[0], "wq": w_peer_q[0].astype(BF16), "keys": peer_sub_keys[0].astype(BF16),
        "u": expert_u[0].astype(BF16), "vt": expert_v[0].astype(BF16).T,
        "g_ple": g_ple[0], "wg": w_ple_gate[0].astype(BF16), "wp": w_ple[0].astype(BF16),
        "g_final": g_final,
    }
    win_bf = w_in[0].astype(BF16)
    wout_bf = w_out[0].astype(BF16)
    sinks = attn_sinks[0]

    xp = x_prompt[0]
    tm = _tile(seq, 256)
    tb = _tile(seq, 512)
    q, k, v, gb, u = _in_proj(xp, g_mix[0], win_bf, jnp.arange(seq, dtype=jnp.int32), tm)
    h1 = _mixer_prompt(q, k, v, gb, u, xp, wout_bf, conv_w[0], sinks, g_attn_out[0], g_conv_out[0])
    y_prompt = _channel_mix(h1, p_prompt[0, 0], w, tm, tb)[None]
    new_k_prompt = k[seq - WINDOW:].reshape(1, 1, WINDOW, N_KV_HEADS, HEAD_DIM)
    new_v_prompt = v[seq - WINDOW:].reshape(1, 1, WINDOW, N_KV_HEADS, HEAD_DIM)
    new_conv_prompt = u[seq - (CONV_K - 1):].reshape(1, 1, CONV_K - 1, CONV_WIDTH)

    xs = x_sample[:, 0]
    pos_s = jnp.full((nb,), PAST_LEN, jnp.int32)
    qs, ks, vs, gbs, us = _in_proj(xs, g_mix[0], win_bf, pos_s, nb)
    attn_s, nk, nv = _attn_sample(qs, ks, vs, cache_k[0].reshape(nb, WINDOW, KV_WIDTH),
                                  cache_v[0].reshape(nb, WINDOW, KV_WIDTH), sinks)
    h1s, ns = _mix_sample(attn_s, gbs, us, state_conv[0].reshape(nb, (CONV_K - 1) * CONV_WIDTH), xs,
                          wout_bf, conv_w[0], g_attn_out[0], g_conv_out[0])
    y_sample = _channel_mix(h1s, p_sample[0, :, 0], w, nb, nb)[:, None]
    new_k_sample = nk.reshape(1, nb, WINDOW, N_KV_HEADS, HEAD_DIM)
    new_v_sample = nv.reshape(1, nb, WINDOW, N_KV_HEADS, HEAD_DIM)
    new_conv_sample = ns.reshape(1, nb, CONV_K - 1, CONV_WIDTH)

    return (y_prompt, y_sample, new_k_prompt, new_v_prompt, new_conv_prompt,
            new_k_sample, new_v_sample, new_conv_sample)
```

```python
import functools

import jax
import jax.numpy as jnp
from jax import lax
from jax.experimental import pallas as pl
from jax.experimental.pallas import tpu as pltpu

F32 = jnp.float32
BF16 = jnp.bfloat16

D_MODEL = 2048
HEAD_DIM = 64
N_HEADS = 16
N_KV_HEADS = 4
GROUP = N_HEADS // N_KV_HEADS
ATTN_WIDTH = N_HEADS * HEAD_DIM
KV_WIDTH = N_KV_HEADS * HEAD_DIM
CONV_WIDTH = D_MODEL - ATTN_WIDTH
CONV_K = 3
WINDOW = 128
ATTN_SCALE = HEAD_DIM ** -0.5
ROPE_THETA = 500000.0
ROPE_DIM = HEAD_DIM // 4
ROPE_HALF = ROPE_DIM // 2
IN_COLS = ATTN_WIDTH + 2 * KV_WIDTH + 3 * CONV_WIDTH
PAST_LEN = 16384
PEER_HEADS = 8
N_KEYS = 128
N_EXPERTS = N_KEYS * N_KEYS
PEER_TOPK = 16
PEER_QDIM = 256
PEER_HALF = PEER_QDIM // 2
PLE_DIM = 256
EPS = 1e-6
NEG = -1e30
NEG_INF = float("-inf")

LANES = 128
SUBLANES = 8
MXU_DIM = 256
VMEM_LIMIT = 56 * 1024 * 1024


def _params(*sem, flags=None):
    return pltpu.CompilerParams(dimension_semantics=sem, vmem_limit_bytes=VMEM_LIMIT, flags=flags)


def _resident(shape):
    zeros = (0,) * len(shape)
    return pl.BlockSpec(shape, lambda *_: zeros, pipeline_mode=pl.Buffered(1))


def _rms(x, g):
    ms = jnp.mean(x * x, axis=-1, keepdims=True)
    return (x * lax.rsqrt(ms + EPS)) * g


def _in_proj_kernel(x_ref, g_ref, w_ref, c_ref, sa_ref, sb_ref,
                    q_ref, k_ref, v_ref, gb_ref, u_ref):
    a = _rms(x_ref[...], g_ref[...])
    z = jnp.dot(a.astype(BF16), w_ref[...], preferred_element_type=F32)
    o_k = ATTN_WIDTH
    o_v = o_k + KV_WIDTH
    o_gb = o_v + KV_WIDTH
    o_gc = o_gb + CONV_WIDTH
    o_xc = o_gc + CONV_WIDTH

    def rope(t):
        width = t.shape[1]
        reps = width // LANES
        c = jnp.concatenate([c_ref[...]] * reps, axis=1)
        sa = jnp.concatenate([sa_ref[...]] * reps, axis=1)
        sb = jnp.concatenate([sb_ref[...]] * reps, axis=1)
        up = pltpu.roll(t, width - ROPE_HALF, axis=1)
        dn = pltpu.roll(t, ROPE_HALF, axis=1)
        return t * c + up * sa + dn * sb

    q_ref[...] = rope(z[:, :o_k])
    k_ref[...] = rope(z[:, o_k:o_v])
    v_ref[...] = z[:, o_v:o_gb]
    gb_ref[...] = z[:, o_gb:o_gc]
    u_ref[...] = z[:, o_gc:o_xc] * z[:, o_xc:]


def _rope_tables(pos):
    inv = ROPE_THETA ** (-jnp.arange(ROPE_HALF, dtype=F32) / ROPE_HALF)
    lane = jnp.arange(LANES) % HEAD_DIM
    inv_lane = jnp.where(lane < ROPE_DIM, inv[lane % ROPE_HALF], 0.0)
    ang = pos.astype(F32)[:, None] * inv_lane[None, :]
    cos, sin = jnp.cos(ang), jnp.sin(ang)
    sa = jnp.where(lane < ROPE_HALF, -sin, 0.0)
    sb = jnp.where(lane >= ROPE_HALF, sin, 0.0)
    return cos, sa, sb


def _in_proj(x, g, w_bf, pos, tm):
    t = x.shape[0]
    c, sa, sb = _rope_tables(pos)
    row = lambda w: pl.BlockSpec((tm, w), lambda i: (i, 0))
    return pl.pallas_call(
        _in_proj_kernel,
        grid=(t // tm,),
        in_specs=[row(D_MODEL), _resident((1, D_MODEL)), _resident((D_MODEL, IN_COLS)),
                  row(LANES), row(LANES), row(LANES)],
        out_specs=[row(ATTN_WIDTH), row(KV_WIDTH), row(KV_WIDTH), row(CONV_WIDTH), row(CONV_WIDTH)],
        out_shape=[jax.ShapeDtypeStruct((t, ATTN_WIDTH), F32),
                   jax.ShapeDtypeStruct((t, KV_WIDTH), F32),
                   jax.ShapeDtypeStruct((t, KV_WIDTH), F32),
                   jax.ShapeDtypeStruct((t, CONV_WIDTH), F32),
                   jax.ShapeDtypeStruct((t, CONV_WIDTH), F32)],
        compiler_params=_params("parallel"),
        name="in_proj",
    )(x, g.reshape(1, D_MODEL), w_bf, c, sa, sb)


def _mix_tail(attn, conv_out, x, ga, gc, wout):
    merged = jnp.concatenate([_rms(attn, ga), _rms(conv_out, gc)], axis=1)
    return x + jnp.dot(merged.astype(BF16), wout, preferred_element_type=F32)


MIX_BLOCKS = 2


def _mixer_prompt_kernel(sinks_ref, q_ref, kp_ref, kc_ref, vp_ref, vc_ref, gb_ref, up_ref, uc_ref,
                         x_ref, wout_ref, convw_ref, ga_ref, gc_ref, h_ref):
    n = pl.program_id(0)
    rows = MIX_BLOCKS * WINDOW
    kall = jnp.concatenate([kp_ref[...], kc_ref[...]], axis=0).astype(BF16)
    vall = jnp.concatenate([vp_ref[...], vc_ref[...]], axis=0).astype(BF16)
    qi = lax.broadcasted_iota(jnp.int32, (WINDOW, 2 * WINDOW), 0) + WINDOW
    kj = lax.broadcasted_iota(jnp.int32, (WINDOW, 2 * WINDOW), 1)
    diff = qi - kj
    band = jnp.where(diff >= 0, jnp.where(diff < WINDOW, 1, 0), 0)
    first = jnp.where(kj + n * rows >= WINDOW, band, 0) > 0
    band = band > 0
    blocks = []
    for j in range(MIX_BLOCKS):
        mask = first if j == 0 else band
        q = q_ref[j * WINDOW:(j + 1) * WINDOW, :]
        kband = kall[j * WINDOW:(j + 2) * WINDOW]
        vband = vall[j * WINDOW:(j + 2) * WINDOW]
        outs = []
        for h in range(N_HEADS):
            g = h // GROUP
            qh = q[:, h * HEAD_DIM:(h + 1) * HEAD_DIM].astype(BF16)
            kg = kband[:, g * HEAD_DIM:(g + 1) * HEAD_DIM]
            vg = vband[:, g * HEAD_DIM:(g + 1) * HEAD_DIM]
            s = lax.dot_general(qh, kg, (((1,), (1,)), ((), ())), preferred_element_type=F32) * ATTN_SCALE
            s = jnp.where(mask, s, NEG)
            sk = sinks_ref[h]
            m = jnp.maximum(jnp.max(s, axis=-1, keepdims=True), sk)
            e = jnp.where(mask, jnp.exp(s - m), 0.0)
            den = jnp.sum(e, axis=-1, keepdims=True) + jnp.exp(sk - m)
            p = e / den
            outs.append(jnp.dot(p.astype(BF16), vg, preferred_element_type=F32))
        blocks.append(jnp.concatenate(outs, axis=1))
    attn = jnp.concatenate(blocks, axis=0)

    u = uc_ref[...]
    prev = jnp.where(n > 0, up_ref[...], 0.0)
    ext = jnp.concatenate([prev, u], axis=0)
    cw = convw_ref[...]
    y = (ext[SUBLANES - 2:SUBLANES - 2 + rows] * cw[0:1]
         + ext[SUBLANES - 1:SUBLANES - 1 + rows] * cw[1:2]
         + u * cw[2:3])
    conv_out = gb_ref[...] * y
    h_ref[...] = _mix_tail(attn, conv_out, x_ref[...], ga_ref[...], gc_ref[...], wout_ref[...])


def _mixer_prompt(q, k, v, gb, u, x, wout_bf, conv_w, sinks, ga, gc):
    t = q.shape[0]
    rows = MIX_BLOCKS * WINDOW
    cur = lambda w: pl.BlockSpec((rows, w), lambda n: (n, 0))
    prv = lambda w: pl.BlockSpec((WINDOW, w), lambda n: (jnp.maximum(n * MIX_BLOCKS - 1, 0), 0))
    tail = pl.BlockSpec((SUBLANES, CONV_WIDTH), lambda n: (jnp.maximum(n * (rows // SUBLANES) - 1, 0), 0))
    return pl.pallas_call(
        _mixer_prompt_kernel,
        grid=(t // rows,),
        in_specs=[pl.BlockSpec(memory_space=pltpu.SMEM),
                  cur(ATTN_WIDTH), prv(KV_WIDTH), cur(KV_WIDTH), prv(KV_WIDTH), cur(KV_WIDTH),
                  cur(CONV_WIDTH), tail, cur(CONV_WIDTH), cur(D_MODEL),
                  _resident((D_MODEL, D_MODEL)), _resident((SUBLANES, CONV_WIDTH)),
                  _resident((1, ATTN_WIDTH)), _resident((1, CONV_WIDTH))],
        out_specs=cur(D_MODEL),
        out_shape=jax.ShapeDtypeStruct((t, D_MODEL), F32),
        compiler_params=_params("parallel"),
        name="mixer_prompt",
    )(sinks, q, k, k, v, v, gb, u, u, x, wout_bf,
      jnp.pad(conv_w, ((0, SUBLANES - CONV_K), (0, 0))), ga.reshape(1, -1), gc.reshape(1, -1))


SAMPLE_ROWS = 8


def _attn_sample_kernel(q_ref, kn_ref, vn_ref, ck_ref, cv_ref, sk_ref, o_ref, nk_ref, nv_ref):
    hl = lax.broadcasted_iota(jnp.int32, (N_HEADS, KV_WIDTH), 1) // HEAD_DIM
    hg = lax.broadcasted_iota(jnp.int32, (N_HEADS, KV_WIDTH), 0) // GROUP
    own = hl == hg
    last = lax.broadcasted_iota(jnp.int32, (WINDOW, KV_WIDTH), 0) == WINDOW - 1
    sk = sk_ref[...][:, 0:1]
    kn = kn_ref[...]
    vn = vn_ref[...]
    for b in range(SAMPLE_ROWS):
        nk = jnp.where(last, kn[b:b + 1], pltpu.roll(ck_ref[b], WINDOW - 1, axis=0))
        nv = jnp.where(last, vn[b:b + 1], pltpu.roll(cv_ref[b], WINDOW - 1, axis=0))
        nk_ref[b] = nk
        nv_ref[b] = nv
        qb = q_ref[b]
        qw = jnp.where(own, jnp.concatenate([qb] * N_KV_HEADS, axis=1), 0.0)
        s = lax.dot_general(qw.astype(BF16), nk.astype(BF16), (((1,), (1,)), ((), ())),
                            preferred_element_type=F32) * ATTN_SCALE
        m = jnp.maximum(jnp.max(s, axis=-1, keepdims=True), sk)
        e = jnp.exp(s - m)
        p = e / (jnp.sum(e, axis=-1, keepdims=True) + jnp.exp(sk - m))
        pv = jnp.where(own, jnp.dot(p.astype(BF16), nv.astype(BF16), preferred_element_type=F32), 0.0)
        o = pv[:, 0:HEAD_DIM]
        for g in range(1, N_KV_HEADS):
            o = o + pv[:, g * HEAD_DIM:(g + 1) * HEAD_DIM]
        o_ref[b] = o


def _attn_sample(q, kn, vn, ck, cv, sinks):
    b = q.shape[0]
    q3 = q.reshape(b, N_HEADS, HEAD_DIM)
    sk = jnp.broadcast_to(sinks.reshape(N_HEADS, 1), (N_HEADS, LANES))
    r = SAMPLE_ROWS
    o3, nk, nv = pl.pallas_call(
        _attn_sample_kernel,
        grid=(b // r,),
        in_specs=[pl.BlockSpec((r, N_HEADS, HEAD_DIM), lambda i: (i, 0, 0)),
                  pl.BlockSpec((r, KV_WIDTH), lambda i: (i, 0)),
                  pl.BlockSpec((r, KV_WIDTH), lambda i: (i, 0)),
                  pl.BlockSpec((r, WINDOW, KV_WIDTH), lambda i: (i, 0, 0)),
                  pl.BlockSpec((r, WINDOW, KV_WIDTH), lambda i: (i, 0, 0)),
                  _resident((N_HEADS, LANES))],
        out_specs=[pl.BlockSpec((r, N_HEADS, HEAD_DIM), lambda i: (i, 0, 0)),
                   pl.BlockSpec((r, WINDOW, KV_WIDTH), lambda i: (i, 0, 0)),
                   pl.BlockSpec((r, WINDOW, KV_WIDTH), lambda i: (i, 0, 0))],
        out_shape=[jax.ShapeDtypeStruct((b, N_HEADS, HEAD_DIM), F32),
                   jax.ShapeDtypeStruct((b, WINDOW, KV_WIDTH), F32),
                   jax.ShapeDtypeStruct((b, WINDOW, KV_WIDTH), F32)],
        compiler_params=_params("parallel"),
        name="attn_sample",
    )(q3, kn, vn, ck, cv, sk)
    return o3.reshape(b, ATTN_WIDTH), nk, nv


def _mix_sample_kernel(attn_ref, gb_ref, u_ref, st_ref, x_ref, wout_ref, convw_ref, ga_ref, gc_ref,
                       h_ref, ns_ref):
    u = u_ref[...]
    st = st_ref[...]
    p0 = st[:, :CONV_WIDTH]
    p1 = st[:, CONV_WIDTH:]
    cw = convw_ref[...]
    y = p0 * cw[0:1] + p1 * cw[1:2] + u * cw[2:3]
    ns_ref[...] = jnp.concatenate([p1, u], axis=1)
    h_ref[...] = _mix_tail(attn_ref[...], gb_ref[...] * y, x_ref[...], ga_ref[...], gc_ref[...],
                           wout_ref[...])


def _mix_sample(attn, gb, u, state2, x, wout_bf, conv_w, ga, gc):
    b = attn.shape[0]
    full = lambda w: pl.BlockSpec((b, w), lambda i: (0, 0))
    return pl.pallas_call(
        _mix_sample_kernel,
        grid=(1,),
        in_specs=[full(ATTN_WIDTH), full(CONV_WIDTH), full(CONV_WIDTH), full(2 * CONV_WIDTH),
                  full(D_MODEL), _resident((D_MODEL, D_MODEL)), _resident((SUBLANES, CONV_WIDTH)),
                  _resident((1, ATTN_WIDTH)), _resident((1, CONV_WIDTH))],
        out_specs=[full(D_MODEL), full(2 * CONV_WIDTH)],
        out_shape=[jax.ShapeDtypeStruct((b, D_MODEL), F32),
                   jax.ShapeDtypeStruct((b, 2 * CONV_WIDTH), F32)],
        compiler_params=_params("arbitrary"),
        name="mix_sample",
    )(attn, gb, u, state2, x, wout_bf,
      jnp.pad(conv_w, ((0, SUBLANES - CONV_K), (0, 0))), ga.reshape(1, -1), gc.reshape(1, -1))


def _peer_query_kernel(h_ref, g_ref, w_ref, q_ref, xbt_ref):
    a = _rms(h_ref[...], g_ref[...])
    xbt_ref[...] = a.T.astype(BF16)
    q_ref[...] = jnp.dot(a.astype(BF16), w_ref[...], preferred_element_type=F32)


def _peer_query(h, g, wq_bf, tm):
    t = h.shape[0]
    row = lambda w: pl.BlockSpec((tm, w), lambda i: (i, 0))
    return pl.pallas_call(
        _peer_query_kernel,
        grid=(t // tm,),
        in_specs=[row(D_MODEL), _resident((1, D_MODEL)), _resident((D_MODEL, PEER_HEADS * PEER_QDIM))],
        out_specs=[row(PEER_HEADS * PEER_QDIM), pl.BlockSpec((D_MODEL, tm), lambda i: (0, i))],
        out_shape=[jax.ShapeDtypeStruct((t, PEER_HEADS * PEER_QDIM), F32),
                   jax.ShapeDtypeStruct((D_MODEL, t), BF16)],
        compiler_params=_params("parallel"),
        name="peer_query",
    )(h, g.reshape(1, D_MODEL), wq_bf)


def _sort16_pairs():
    def merge(lo, hi, r):
        step = r * 2
        if step < hi - lo:
            yield from merge(lo, hi, step)
            yield from merge(lo + r, hi, step)
            yield from [(i, i + r) for i in range(lo + r, hi - r, step)]
        else:
            yield (lo, lo + r)

    def sort(lo, hi):
        if hi - lo >= 1:
            mid = lo + (hi - lo) // 2
            yield from sort(lo, mid)
            yield from sort(mid + 1, hi)
            yield from merge(lo, hi, 1)

    return list(sort(0, PEER_TOPK - 1))


def _top_sorted(s, with_code=False):
    n_v = N_KEYS // SUBLANES
    assert n_v == PEER_TOPK
    tops, codes = [], []
    for c in range(s.shape[1] // LANES):
        cols = slice(c * LANES, (c + 1) * LANES)
        x = [s[i * SUBLANES:(i + 1) * SUBLANES, cols] for i in range(n_v)]
        v = list(x)

        def exchange(i, j):
            v[i], v[j] = jnp.maximum(v[i], v[j]), jnp.minimum(v[i], v[j])

        for i, j in _sort16_pairs():
            exchange(i, j)
        shift = SUBLANES // 2
        while shift >= 1:
            rolled = [pltpu.roll(t, shift, axis=0) for t in v]
            v = [jnp.maximum(v[i], rolled[n_v - 1 - i]) for i in range(n_v)]
            d = n_v // 2
            while d >= 1:
                for i in range(n_v):
                    if not i & d:
                        exchange(i, i + d)
                d //= 2
            shift //= 2
        tops.append(jnp.concatenate([t[0:1] for t in v], axis=0))
        if with_code:
            parts = []
            for xi in x:
                cnt = jnp.where(xi >= v[0], 1.0, 0.0)
                for t in v[1:]:
                    cnt = cnt + jnp.where(xi >= t, 1.0, 0.0)
                parts.append(cnt)
            codes.append(jnp.concatenate(parts, axis=0))
    top = jnp.concatenate(tops, axis=1)
    return top, (jnp.concatenate(codes, axis=1) if with_code else None)


def _peer_select_kernel(q_ref, keys_ref, thr_ref, a_ref, code_ref, b_ref):
    tb = q_ref.shape[0]
    q = q_ref[...].astype(BF16)
    for h in range(PEER_HEADS):
        sc = []
        for c in range(2):
            col = (2 * h + c) * PEER_HALF
            sc.append(lax.dot_general(keys_ref[h, c], q[:, col:col + PEER_HALF],
                                      (((1,), (1,)), ((), ())), preferred_element_type=F32))
        s1, s2 = sc
        t1, _ = _top_sorted(s1)
        t2, code2 = _top_sorted(s2, with_code=True)
        r16 = lax.broadcasted_iota(jnp.int32, (16, tb), 0)
        r8 = lax.broadcasted_iota(jnp.int32, (8, tb), 0)
        pieces = [t1[0:1] + t2]
        for r in range(1, 5):
            pieces.append(t1[r:r + 1] + t2[0:8])
        pieces.append(jnp.where(r16 >= 5, t1 + t2[0:1], NEG_INF))
        pieces.append(jnp.where(r8 >= 5, t1[0:8] + t2[1:2], NEG_INF))
        cand = jnp.concatenate(pieces, axis=0)
        work = cand
        for _ in range(PEER_TOPK):
            tau = jnp.max(work, axis=0, keepdims=True)
            work = jnp.where(work >= tau, NEG_INF, work)
        top = t1[0:1] + t2[0:1]
        z = jnp.sum(jnp.where(cand >= tau, jnp.exp(cand - top), 0.0), axis=0, keepdims=True)
        thr_rank = jnp.full((PEER_TOPK, tb), float(PEER_TOPK + 1), F32)
        for c in range(PEER_TOPK):
            thr_rank = thr_rank - jnp.where(t1 + t2[c:c + 1] >= tau, 1.0, 0.0)
        thr = jnp.full((N_KEYS, tb), float(PEER_TOPK + 1), F32)
        for r in range(PEER_TOPK):
            thr = jnp.where(s1 == t1[r:r + 1], thr_rank[r:r + 1], thr)
        thr_ref[h] = thr
        a_ref[h] = jnp.exp(s1 - t1[0:1]) / z
        code_ref[h] = code2.astype(BF16)
        b_ref[h] = jnp.exp(s2 - t2[0:1]).astype(BF16)


def _peer_select(q, keys_bf, tb):
    t = q.shape[0]
    big = pl.BlockSpec((PEER_HEADS, N_KEYS, tb), lambda i: (0, 0, i))
    big_shape = jax.ShapeDtypeStruct((PEER_HEADS, N_KEYS, t), F32)
    return pl.pallas_call(
        _peer_select_kernel,
        grid=(t // tb,),
        in_specs=[pl.BlockSpec((tb, PEER_HEADS * PEER_QDIM), lambda i: (i, 0)),
                  _resident((PEER_HEADS, 2, N_KEYS, PEER_HALF))],
        out_specs=[big, big, big, big],
        out_shape=[big_shape, big_shape, jax.ShapeDtypeStruct((PEER_HEADS, N_KEYS, t), BF16),
                   jax.ShapeDtypeStruct((PEER_HEADS, N_KEYS, t), BF16)],
        compiler_params=_params("parallel"),
        name="peer_select",
    )(q, keys_bf)


EXPERT_CHUNK = 1024
KEY_ROWS = EXPERT_CHUNK // N_KEYS


def _gelu(x):
    return 0.5 * x * (1.0 + lax.erf(x * (2.0 ** -0.5)))


HALF_CHUNK = EXPERT_CHUNK // 2
HALF_ROWS = KEY_ROWS // 2


def _peer_experts_kernel(u_ref, xbt_ref, vt_ref, thr_ref, a_ref, code_ref, b_ref,
                         o_ref, acc_ref, ht0_ref, ht1_ref, g_ref, *, n_chunks):
    s = pl.program_id(0)
    e = lax.rem(jnp.maximum(s - 1, 0), n_chunks)

    @pl.when(s == 0)
    def _():
        ht1_ref[...] = jnp.zeros_like(ht1_ref)

    @pl.when(e == 0)
    def _():
        acc_ref[...] = jnp.zeros_like(acc_ref)

    tb = xbt_ref.shape[1]
    tok_tile = min(MXU_DIM, tb)
    n_tok_tiles = tb // tok_tile

    def hidden(ht_out, n):
        cols = slice(n * tok_tile, (n + 1) * tok_tile)
        ht_out[:, cols] = jnp.dot(u_ref[...], xbt_ref[:, cols], preferred_element_type=F32)

    def spread(row):
        return jnp.broadcast_to(row.astype(BF16), (N_KEYS, tb))

    def gate(ht_in, r):
        w = None
        for h in range(PEER_HEADS):
            term = jnp.where(code_ref[h] >= spread(thr_ref[h, r:r + 1, :]),
                             b_ref[h] * spread(a_ref[h, r:r + 1, :]), jnp.zeros((), BF16))
            w = term if w is None else w + term
        rows = slice(r * N_KEYS, (r + 1) * N_KEYS)
        g_ref[rows, :] = _gelu(ht_in[rows, :]).astype(BF16) * w

    def step(ht_in, ht_out):
        for p in range(2):
            if p == 1:
                pl.delay(1)
            for n in range(n_tok_tiles):
                if n * 2 // n_tok_tiles == p:
                    hidden(ht_out, n)
            for r in range(p * HALF_ROWS, (p + 1) * HALF_ROWS):
                gate(ht_in, r)
            rows = slice(p * HALF_CHUNK, (p + 1) * HALF_CHUNK)
            acc_ref[...] += jnp.dot(vt_ref[:, rows], g_ref[rows, :], preferred_element_type=F32)

    @pl.when(lax.rem(s, 2) == 0)
    def _():
        step(ht1_ref, ht0_ref)

    @pl.when(lax.rem(s, 2) == 1)
    def _():
        step(ht0_ref, ht1_ref)

    @pl.when(jnp.logical_and(e == n_chunks - 1, s > 0))
    def _():
        o_ref[...] = acc_ref[...].T


def _transpose_cast_kernel(v_ref, o_ref):
    o_ref[...] = v_ref[...].T.astype(BF16)


def _transposed_bf16(v):
    e, d = v.shape
    return pl.pallas_call(
        _transpose_cast_kernel,
        grid=(e // EXPERT_CHUNK,),
        in_specs=[pl.BlockSpec((EXPERT_CHUNK, d), lambda i: (i, 0))],
        out_specs=pl.BlockSpec((d, EXPERT_CHUNK), lambda i: (0, i)),
        out_shape=jax.ShapeDtypeStruct((d, e), BF16),
        compiler_params=_params("parallel"),
        name="expert_v_layout",
    )(v)


def _peer_experts(xbt, u_bf, vt_bf, thr, a, s2, b, tb):
    t = xbt.shape[1]
    n_chunks = N_EXPERTS // EXPERT_CHUNK
    n = (t // tb) * n_chunks
    lag = lambda s, d: jnp.clip(s - d, 0, n - 1)
    chunk = lambda s, d: lax.rem(lag(s, d), n_chunks)
    block = lambda s, d: lag(s, d) // n_chunks
    rows = pl.BlockSpec((PEER_HEADS, KEY_ROWS, tb), lambda s: (0, chunk(s, 1), block(s, 1)))
    full = pl.BlockSpec((PEER_HEADS, N_KEYS, tb), lambda s: (0, 0, block(s, 1)))
    return pl.pallas_call(
        functools.partial(_peer_experts_kernel, n_chunks=n_chunks),
        grid=(n + 1,),
        in_specs=[pl.BlockSpec((EXPERT_CHUNK, D_MODEL), lambda s: (chunk(s, 0), 0)),
                  pl.BlockSpec((D_MODEL, tb), lambda s: (0, block(s, 0))),
                  pl.BlockSpec((D_MODEL, EXPERT_CHUNK), lambda s: (0, chunk(s, 1))),
                  rows, rows, full, full],
        out_specs=pl.BlockSpec((tb, D_MODEL), lambda s: (block(s, 1), 0)),
        out_shape=jax.ShapeDtypeStruct((t, D_MODEL), F32),
        scratch_shapes=[pltpu.VMEM((D_MODEL, tb), F32), pltpu.VMEM((EXPERT_CHUNK, tb), F32),
                        pltpu.VMEM((EXPERT_CHUNK, tb), F32), pltpu.VMEM((EXPERT_CHUNK, tb), BF16)],
        compiler_params=_params("arbitrary"),
        name="peer_experts",
    )(u_bf, xbt, vt_bf, thr, a, s2, b)


def _final_kernel(h_ref, y_ref, p_ref, gp_ref, wg_ref, wp_ref, gf_ref, o_ref):
    h = h_ref[...] + y_ref[...]
    ple = jnp.dot(p_ref[...].astype(BF16), wp_ref[...], preferred_element_type=F32)
    gate = jax.nn.sigmoid(jnp.dot(_rms(h, gp_ref[...]).astype(BF16), wg_ref[...],
                                  preferred_element_type=F32))
    o_ref[...] = _rms(h + ple * gate, gf_ref[...])


def _final(h, y, p, g_ple, wg_bf, wp_bf, g_final, tm):
    t = h.shape[0]
    row = lambda w: pl.BlockSpec((tm, w), lambda i: (i, 0))
    return pl.pallas_call(
        _final_kernel,
        grid=(t // tm,),
        in_specs=[row(D_MODEL), row(D_MODEL), row(PLE_DIM), _resident((1, D_MODEL)),
                  _resident((D_MODEL, D_MODEL)), _resident((PLE_DIM, D_MODEL)), _resident((1, D_MODEL))],
        out_specs=row(D_MODEL),
        out_shape=jax.ShapeDtypeStruct((t, D_MODEL), F32),
        compiler_params=_params("parallel"),
        name="final",
    )(h, y, p, g_ple.reshape(1, D_MODEL), wg_bf, wp_bf, g_final.reshape(1, D_MODEL))


def _tile(t, pref):
    return pref if t % pref == 0 else t


def _channel_mix(h1, p, w, tm, tb):
    q, xbt = _peer_query(h1, w["g_ffn"], w["wq"], tm)
    thr, a, s2, b = _peer_select(q, w["keys"], tb)
    y = _peer_experts(xbt, w["u"], w["vt"], thr, a, s2, b, tb)
    return _final(h1, y, p, w["g_ple"], w["wg"], w["wp"], w["g_final"], tm)


def kernel(x_prompt, x_sample, cache_k, cache_v, state_conv, p_prompt, p_sample, g_mix, w_in, conv_w,
           attn_sinks, g_attn_out, g_conv_out, w_out, g_ffn, w_peer_q, peer_sub_keys, expert_u,
           expert_v, g_ple, w_ple_gate, w_ple, g_final):
    seq = x_prompt.shape[1]
    nb = x_sample.shape[0]
    w = {
        "g_ffn": g_ffn[0], "wq": w_peer_q[0].astype(BF16), "keys": peer_sub_keys[0].astype(BF16),
        "u": expert_u[0].astype(BF16), "vt": _transposed_bf16(expert_v[0]),
        "g_ple": g_ple[0], "wg": w_ple_gate[0].astype(BF16), "wp": w_ple[0].astype(BF16),
        "g_final": g_final,
    }
    win_bf = w_in[0].astype(BF16)
    wout_bf = w_out[0].astype(BF16)
    sinks = attn_sinks[0]

    xp = x_prompt[0]
    tm = _tile(seq, 512)
    tb = _tile(seq, 512)
    q, k, v, gb, u = _in_proj(xp, g_mix[0], win_bf, jnp.arange(seq, dtype=jnp.int32), tm)
    h1 = _mixer_prompt(q, k, v, gb, u, xp, wout_bf, conv_w[0], sinks, g_attn_out[0], g_conv_out[0])
    y_prompt = _channel_mix(h1, p_prompt[0, 0], w, tm, tb)[None]
    new_k_prompt = k[seq - WINDOW:].reshape(1, 1, WINDOW, N_KV_HEADS, HEAD_DIM)
    new_v_prompt = v[seq - WINDOW:].reshape(1, 1, WINDOW, N_KV_HEADS, HEAD_DIM)
    new_conv_prompt = u[seq - (CONV_K - 1):].reshape(1, 1, CONV_K - 1, CONV_WIDTH)

    xs = x_sample[:, 0]
    pos_s = jnp.full((nb,), PAST_LEN, jnp.int32)
    qs, ks, vs, gbs, us = _in_proj(xs, g_mix[0], win_bf, pos_s, nb)
    attn_s, nk, nv = _attn_sample(qs, ks, vs, cache_k[0].reshape(nb, WINDOW, KV_WIDTH),
                                  cache_v[0].reshape(nb, WINDOW, KV_WIDTH), sinks)
    h1s, ns = _mix_sample(attn_s, gbs, us, state_conv[0].reshape(nb, (CONV_K - 1) * CONV_WIDTH), xs,
                          wout_bf, conv_w[0], g_attn_out[0], g_conv_out[0])
    y_sample = _channel_mix(h1s, p_sample[0, :, 0], w, nb, nb)[:, None]
    new_k_sample = nk.reshape(1, nb, WINDOW, N_KV_HEADS, HEAD_DIM)
    new_v_sample = nv.reshape(1, nb, WINDOW, N_KV_HEADS, HEAD_DIM)
    new_conv_sample = ns.reshape(1, nb, CONV_K - 1, CONV_WIDTH)

    return (y_prompt, y_sample, new_k_prompt, new_v_prompt, new_conv_prompt,
            new_k_sample, new_v_sample, new_conv_sample)
```

```python
import functools

import jax
import jax.numpy as jnp
from jax import lax
from jax.experimental import pallas as pl
from jax.experimental.pallas import tpu as pltpu

F32 = jnp.float32
BF16 = jnp.bfloat16

D_MODEL = 2048
HEAD_DIM = 64
N_HEADS = 16
N_KV_HEADS = 4
GROUP = N_HEADS // N_KV_HEADS
ATTN_WIDTH = N_HEADS * HEAD_DIM
KV_WIDTH = N_KV_HEADS * HEAD_DIM
CONV_WIDTH = D_MODEL - ATTN_WIDTH
CONV_K = 3
WINDOW = 128
ATTN_SCALE = HEAD_DIM ** -0.5
ROPE_THETA = 500000.0
ROPE_DIM = HEAD_DIM // 4
ROPE_HALF = ROPE_DIM // 2
IN_COLS = ATTN_WIDTH + 2 * KV_WIDTH + 3 * CONV_WIDTH
PAST_LEN = 16384
PEER_HEADS = 8
N_KEYS = 128
N_EXPERTS = N_KEYS * N_KEYS
PEER_TOPK = 16
PEER_QDIM = 256
PEER_HALF = PEER_QDIM // 2
PLE_DIM = 256
EPS = 1e-6
NEG = -1e30
NEG_INF = float("-inf")

LANES = 128
SUBLANES = 8
VMEM_LIMIT = 56 * 1024 * 1024


def _params(*sem, flags=None):
    return pltpu.CompilerParams(dimension_semantics=sem, vmem_limit_bytes=VMEM_LIMIT, flags=flags)


def _resident(shape):
    zeros = (0,) * len(shape)
    return pl.BlockSpec(shape, lambda *_: zeros, pipeline_mode=pl.Buffered(1))


def _rms(x, g):
    ms = jnp.mean(x * x, axis=-1, keepdims=True)
    return (x * lax.rsqrt(ms + EPS)) * g


def _in_proj_kernel(x_ref, g_ref, w_ref, c_ref, sa_ref, sb_ref,
                    q_ref, k_ref, v_ref, gb_ref, u_ref):
    a = _rms(x_ref[...], g_ref[...])
    z = jnp.dot(a.astype(BF16), w_ref[...], preferred_element_type=F32)
    o_k = ATTN_WIDTH
    o_v = o_k + KV_WIDTH
    o_gb = o_v + KV_WIDTH
    o_gc = o_gb + CONV_WIDTH
    o_xc = o_gc + CONV_WIDTH

    def rope(t):
        width = t.shape[1]
        reps = width // LANES
        c = jnp.concatenate([c_ref[...]] * reps, axis=1)
        sa = jnp.concatenate([sa_ref[...]] * reps, axis=1)
        sb = jnp.concatenate([sb_ref[...]] * reps, axis=1)
        up = pltpu.roll(t, width - ROPE_HALF, axis=1)
        dn = pltpu.roll(t, ROPE_HALF, axis=1)
        return t * c + up * sa + dn * sb

    q_ref[...] = rope(z[:, :o_k])
    k_ref[...] = rope(z[:, o_k:o_v])
    v_ref[...] = z[:, o_v:o_gb]
    gb_ref[...] = z[:, o_gb:o_gc]
    u_ref[...] = z[:, o_gc:o_xc] * z[:, o_xc:]


def _rope_tables(pos):
    inv = ROPE_THETA ** (-jnp.arange(ROPE_HALF, dtype=F32) / ROPE_HALF)
    lane = jnp.arange(LANES) % HEAD_DIM
    inv_lane = jnp.where(lane < ROPE_DIM, inv[lane % ROPE_HALF], 0.0)
    ang = pos.astype(F32)[:, None] * inv_lane[None, :]
    cos, sin = jnp.cos(ang), jnp.sin(ang)
    sa = jnp.where(lane < ROPE_HALF, -sin, 0.0)
    sb = jnp.where(lane >= ROPE_HALF, sin, 0.0)
    return cos, sa, sb


def _in_proj(x, g, w_bf, pos, tm):
    t = x.shape[0]
    c, sa, sb = _rope_tables(pos)
    row = lambda w: pl.BlockSpec((tm, w), lambda i: (i, 0))
    return pl.pallas_call(
        _in_proj_kernel,
        grid=(t // tm,),
        in_specs=[row(D_MODEL), _resident((1, D_MODEL)), _resident((D_MODEL, IN_COLS)),
                  row(LANES), row(LANES), row(LANES)],
        out_specs=[row(ATTN_WIDTH), row(KV_WIDTH), row(KV_WIDTH), row(CONV_WIDTH), row(CONV_WIDTH)],
        out_shape=[jax.ShapeDtypeStruct((t, ATTN_WIDTH), F32),
                   jax.ShapeDtypeStruct((t, KV_WIDTH), F32),
                   jax.ShapeDtypeStruct((t, KV_WIDTH), F32),
                   jax.ShapeDtypeStruct((t, CONV_WIDTH), F32),
                   jax.ShapeDtypeStruct((t, CONV_WIDTH), F32)],
        compiler_params=_params("parallel"),
        name="in_proj",
    )(x, g.reshape(1, D_MODEL), w_bf, c, sa, sb)


def _mix_tail(attn, conv_out, x, ga, gc, wout):
    merged = jnp.concatenate([_rms(attn, ga), _rms(conv_out, gc)], axis=1)
    return x + jnp.dot(merged.astype(BF16), wout, preferred_element_type=F32)


MIX_BLOCKS = 2


def _mixer_prompt_kernel(sinks_ref, q_ref, kp_ref, kc_ref, vp_ref, vc_ref, gb_ref, up_ref, uc_ref,
                         x_ref, wout_ref, convw_ref, ga_ref, gc_ref, h_ref):
    n = pl.program_id(0)
    rows = MIX_BLOCKS * WINDOW
    kall = jnp.concatenate([kp_ref[...], kc_ref[...]], axis=0).astype(BF16)
    vall = jnp.concatenate([vp_ref[...], vc_ref[...]], axis=0).astype(BF16)
    qi = lax.broadcasted_iota(jnp.int32, (WINDOW, 2 * WINDOW), 0) + WINDOW
    kj = lax.broadcasted_iota(jnp.int32, (WINDOW, 2 * WINDOW), 1)
    diff = qi - kj
    band = jnp.where(diff >= 0, jnp.where(diff < WINDOW, 1, 0), 0)
    first = jnp.where(kj + n * rows >= WINDOW, band, 0) > 0
    band = band > 0
    blocks = []
    for j in range(MIX_BLOCKS):
        mask = first if j == 0 else band
        q = q_ref[j * WINDOW:(j + 1) * WINDOW, :]
        kband = kall[j * WINDOW:(j + 2) * WINDOW]
        vband = vall[j * WINDOW:(j + 2) * WINDOW]
        outs = []
        for h in range(N_HEADS):
            g = h // GROUP
            qh = q[:, h * HEAD_DIM:(h + 1) * HEAD_DIM].astype(BF16)
            kg = kband[:, g * HEAD_DIM:(g + 1) * HEAD_DIM]
            vg = vband[:, g * HEAD_DIM:(g + 1) * HEAD_DIM]
            s = lax.dot_general(qh, kg, (((1,), (1,)), ((), ())), preferred_element_type=F32) * ATTN_SCALE
            s = jnp.where(mask, s, NEG)
            sk = sinks_ref[h]
            m = jnp.maximum(jnp.max(s, axis=-1, keepdims=True), sk)
            e = jnp.where(mask, jnp.exp(s - m), 0.0)
            den = jnp.sum(e, axis=-1, keepdims=True) + jnp.exp(sk - m)
            p = e / den
            outs.append(jnp.dot(p.astype(BF16), vg, preferred_element_type=F32))
        blocks.append(jnp.concatenate(outs, axis=1))
    attn = jnp.concatenate(blocks, axis=0)

    u = uc_ref[...]
    prev = jnp.where(n > 0, up_ref[...], 0.0)
    ext = jnp.concatenate([prev, u], axis=0)
    cw = convw_ref[...]
    y = (ext[SUBLANES - 2:SUBLANES - 2 + rows] * cw[0:1]
         + ext[SUBLANES - 1:SUBLANES - 1 + rows] * cw[1:2]
         + u * cw[2:3])
    conv_out = gb_ref[...] * y
    h_ref[...] = _mix_tail(attn, conv_out, x_ref[...], ga_ref[...], gc_ref[...], wout_ref[...])


def _mixer_prompt(q, k, v, gb, u, x, wout_bf, conv_w, sinks, ga, gc):
    t = q.shape[0]
    rows = MIX_BLOCKS * WINDOW
    cur = lambda w: pl.BlockSpec((rows, w), lambda n: (n, 0))
    prv = lambda w: pl.BlockSpec((WINDOW, w), lambda n: (jnp.maximum(n * MIX_BLOCKS - 1, 0), 0))
    tail = pl.BlockSpec((SUBLANES, CONV_WIDTH), lambda n: (jnp.maximum(n * (rows // SUBLANES) - 1, 0), 0))
    return pl.pallas_call(
        _mixer_prompt_kernel,
        grid=(t // rows,),
        in_specs=[pl.BlockSpec(memory_space=pltpu.SMEM),
                  cur(ATTN_WIDTH), prv(KV_WIDTH), cur(KV_WIDTH), prv(KV_WIDTH), cur(KV_WIDTH),
                  cur(CONV_WIDTH), tail, cur(CONV_WIDTH), cur(D_MODEL),
                  _resident((D_MODEL, D_MODEL)), _resident((SUBLANES, CONV_WIDTH)),
                  _resident((1, ATTN_WIDTH)), _resident((1, CONV_WIDTH))],
        out_specs=cur(D_MODEL),
        out_shape=jax.ShapeDtypeStruct((t, D_MODEL), F32),
        compiler_params=_params("parallel"),
        name="mixer_prompt",
    )(sinks, q, k, k, v, v, gb, u, u, x, wout_bf,
      jnp.pad(conv_w, ((0, SUBLANES - CONV_K), (0, 0))), ga.reshape(1, -1), gc.reshape(1, -1))


SAMPLE_ROWS = 8


def _attn_sample_kernel(q_ref, kn_ref, vn_ref, ck_ref, cv_ref, sk_ref, o_ref, nk_ref, nv_ref):
    hl = lax.broadcasted_iota(jnp.int32, (N_HEADS, KV_WIDTH), 1) // HEAD_DIM
    hg = lax.broadcasted_iota(jnp.int32, (N_HEADS, KV_WIDTH), 0) // GROUP
    own = hl == hg
    last = lax.broadcasted_iota(jnp.int32, (WINDOW, KV_WIDTH), 0) == WINDOW - 1
    sk = sk_ref[...][:, 0:1]
    kn = kn_ref[...]
    vn = vn_ref[...]
    for b in range(SAMPLE_ROWS):
        nk = jnp.where(last, kn[b:b + 1], pltpu.roll(ck_ref[b], WINDOW - 1, axis=0))
        nv = jnp.where(last, vn[b:b + 1], pltpu.roll(cv_ref[b], WINDOW - 1, axis=0))
        nk_ref[b] = nk
        nv_ref[b] = nv
        qb = q_ref[b]
        qw = jnp.where(own, jnp.concatenate([qb] * N_KV_HEADS, axis=1), 0.0)
        s = lax.dot_general(qw.astype(BF16), nk.astype(BF16), (((1,), (1,)), ((), ())),
                            preferred_element_type=F32) * ATTN_SCALE
        m = jnp.maximum(jnp.max(s, axis=-1, keepdims=True), sk)
        e = jnp.exp(s - m)
        p = e / (jnp.sum(e, axis=-1, keepdims=True) + jnp.exp(sk - m))
        pv = jnp.where(own, jnp.dot(p.astype(BF16), nv.astype(BF16), preferred_element_type=F32), 0.0)
        o = pv[:, 0:HEAD_DIM]
        for g in range(1, N_KV_HEADS):
            o = o + pv[:, g * HEAD_DIM:(g + 1) * HEAD_DIM]
        o_ref[b] = o


def _attn_sample(q, kn, vn, ck, cv, sinks):
    b = q.shape[0]
    q3 = q.reshape(b, N_HEADS, HEAD_DIM)
    sk = jnp.broadcast_to(sinks.reshape(N_HEADS, 1), (N_HEADS, LANES))
    r = SAMPLE_ROWS
    o3, nk, nv = pl.pallas_call(
        _attn_sample_kernel,
        grid=(b // r,),
        in_specs=[pl.BlockSpec((r, N_HEADS, HEAD_DIM), lambda i: (i, 0, 0)),
                  pl.BlockSpec((r, KV_WIDTH), lambda i: (i, 0)),
                  pl.BlockSpec((r, KV_WIDTH), lambda i: (i, 0)),
                  pl.BlockSpec((r, WINDOW, KV_WIDTH), lambda i: (i, 0, 0)),
                  pl.BlockSpec((r, WINDOW, KV_WIDTH), lambda i: (i, 0, 0)),
                  _resident((N_HEADS, LANES))],
        out_specs=[pl.BlockSpec((r, N_HEADS, HEAD_DIM), lambda i: (i, 0, 0)),
                   pl.BlockSpec((r, WINDOW, KV_WIDTH), lambda i: (i, 0, 0)),
                   pl.BlockSpec((r, WINDOW, KV_WIDTH), lambda i: (i, 0, 0))],
        out_shape=[jax.ShapeDtypeStruct((b, N_HEADS, HEAD_DIM), F32),
                   jax.ShapeDtypeStruct((b, WINDOW, KV_WIDTH), F32),
                   jax.ShapeDtypeStruct((b, WINDOW, KV_WIDTH), F32)],
        compiler_params=_params("parallel"),
        name="attn_sample",
    )(q3, kn, vn, ck, cv, sk)
    return o3.reshape(b, ATTN_WIDTH), nk, nv


def _mix_sample_kernel(attn_ref, gb_ref, u_ref, st_ref, x_ref, wout_ref, convw_ref, ga_ref, gc_ref,
                       h_ref, ns_ref):
    u = u_ref[...]
    st = st_ref[...]
    p0 = st[:, :CONV_WIDTH]
    p1 = st[:, CONV_WIDTH:]
    cw = convw_ref[...]
    y = p0 * cw[0:1] + p1 * cw[1:2] + u * cw[2:3]
    ns_ref[...] = jnp.concatenate([p1, u], axis=1)
    h_ref[...] = _mix_tail(attn_ref[...], gb_ref[...] * y, x_ref[...], ga_ref[...], gc_ref[...],
                           wout_ref[...])


def _mix_sample(attn, gb, u, state2, x, wout_bf, conv_w, ga, gc):
    b = attn.shape[0]
    full = lambda w: pl.BlockSpec((b, w), lambda i: (0, 0))
    return pl.pallas_call(
        _mix_sample_kernel,
        grid=(1,),
        in_specs=[full(ATTN_WIDTH), full(CONV_WIDTH), full(CONV_WIDTH), full(2 * CONV_WIDTH),
                  full(D_MODEL), _resident((D_MODEL, D_MODEL)), _resident((SUBLANES, CONV_WIDTH)),
                  _resident((1, ATTN_WIDTH)), _resident((1, CONV_WIDTH))],
        out_specs=[full(D_MODEL), full(2 * CONV_WIDTH)],
        out_shape=[jax.ShapeDtypeStruct((b, D_MODEL), F32),
                   jax.ShapeDtypeStruct((b, 2 * CONV_WIDTH), F32)],
        compiler_params=_params("arbitrary"),
        name="mix_sample",
    )(attn, gb, u, state2, x, wout_bf,
      jnp.pad(conv_w, ((0, SUBLANES - CONV_K), (0, 0))), ga.reshape(1, -1), gc.reshape(1, -1))


def _peer_query_kernel(h_ref, g_ref, w_ref, q_ref, xbt_ref):
    a = _rms(h_ref[...], g_ref[...])
    xbt_ref[...] = a.T.astype(BF16)
    q_ref[...] = jnp.dot(a.astype(BF16), w_ref[...], preferred_element_type=F32)


def _peer_query(h, g, wq_bf, tm):
    t = h.shape[0]
    row = lambda w: pl.BlockSpec((tm, w), lambda i: (i, 0))
    return pl.pallas_call(
        _peer_query_kernel,
        grid=(t // tm,),
        in_specs=[row(D_MODEL), _resident((1, D_MODEL)), _resident((D_MODEL, PEER_HEADS * PEER_QDIM))],
        out_specs=[row(PEER_HEADS * PEER_QDIM), pl.BlockSpec((D_MODEL, tm), lambda i: (0, i))],
        out_shape=[jax.ShapeDtypeStruct((t, PEER_HEADS * PEER_QDIM), F32),
                   jax.ShapeDtypeStruct((D_MODEL, t), BF16)],
        compiler_params=_params("parallel"),
        name="peer_query",
    )(h, g.reshape(1, D_MODEL), wq_bf)


def _sort16_pairs():
    def merge(lo, hi, r):
        step = r * 2
        if step < hi - lo:
            yield from merge(lo, hi, step)
            yield from merge(lo + r, hi, step)
            yield from [(i, i + r) for i in range(lo + r, hi - r, step)]
        else:
            yield (lo, lo + r)

    def sort(lo, hi):
        if hi - lo >= 1:
            mid = lo + (hi - lo) // 2
            yield from sort(lo, mid)
            yield from sort(mid + 1, hi)
            yield from merge(lo, hi, 1)

    return list(sort(0, PEER_TOPK - 1))


def _top_sorted(s, with_code=False):
    n_v = N_KEYS // SUBLANES
    assert n_v == PEER_TOPK
    tops, codes = [], []
    for c in range(s.shape[1] // LANES):
        cols = slice(c * LANES, (c + 1) * LANES)
        x = [s[i * SUBLANES:(i + 1) * SUBLANES, cols] for i in range(n_v)]
        v = list(x)

        def exchange(i, j):
            v[i], v[j] = jnp.maximum(v[i], v[j]), jnp.minimum(v[i], v[j])

        for i, j in _sort16_pairs():
            exchange(i, j)
        shift = SUBLANES // 2
        while shift >= 1:
            rolled = [pltpu.roll(t, shift, axis=0) for t in v]
            v = [jnp.maximum(v[i], rolled[n_v - 1 - i]) for i in range(n_v)]
            d = n_v // 2
            while d >= 1:
                for i in range(n_v):
                    if not i & d:
                        exchange(i, i + d)
                d //= 2
            shift //= 2
        tops.append(jnp.concatenate([t[0:1] for t in v], axis=0))
        if with_code:
            parts = []
            for xi in x:
                cnt = jnp.where(xi >= v[0], 1.0, 0.0)
                for t in v[1:]:
                    cnt = cnt + jnp.where(xi >= t, 1.0, 0.0)
                parts.append(cnt)
            codes.append(jnp.concatenate(parts, axis=0))
    top = jnp.concatenate(tops, axis=1)
    return top, (jnp.concatenate(codes, axis=1) if with_code else None)


def _peer_select_kernel(q_ref, keys_ref, thr_ref, a_ref, code_ref, b_ref):
    tb = q_ref.shape[0]
    q = q_ref[...].astype(BF16)
    for h in range(PEER_HEADS):
        sc = []
        for c in range(2):
            col = (2 * h + c) * PEER_HALF
            sc.append(lax.dot_general(keys_ref[h, c], q[:, col:col + PEER_HALF],
                                      (((1,), (1,)), ((), ())), preferred_element_type=F32))
        s1, s2 = sc
        t1, _ = _top_sorted(s1)
        t2, code2 = _top_sorted(s2, with_code=True)
        r16 = lax.broadcasted_iota(jnp.int32, (16, tb), 0)
        r8 = lax.broadcasted_iota(jnp.int32, (8, tb), 0)
        pieces = [t1[0:1] + t2]
        for r in range(1, 5):
            pieces.append(t1[r:r + 1] + t2[0:8])
        pieces.append(jnp.where(r16 >= 5, t1 + t2[0:1], NEG_INF))
        pieces.append(jnp.where(r8 >= 5, t1[0:8] + t2[1:2], NEG_INF))
        cand = jnp.concatenate(pieces, axis=0)
        work = cand
        for _ in range(PEER_TOPK):
            tau = jnp.max(work, axis=0, keepdims=True)
            work = jnp.where(work >= tau, NEG_INF, work)
        top = t1[0:1] + t2[0:1]
        z = jnp.sum(jnp.where(cand >= tau, jnp.exp(cand - top), 0.0), axis=0, keepdims=True)
        thr_rank = jnp.full((PEER_TOPK, tb), float(PEER_TOPK + 1), F32)
        for c in range(PEER_TOPK):
            thr_rank = thr_rank - jnp.where(t1 + t2[c:c + 1] >= tau, 1.0, 0.0)
        thr = jnp.full((N_KEYS, tb), float(PEER_TOPK + 1), F32)
        for r in range(PEER_TOPK):
            thr = jnp.where(s1 == t1[r:r + 1], thr_rank[r:r + 1], thr)
        thr_ref[h] = thr
        a_ref[h] = jnp.exp(s1 - t1[0:1]) / z
        code_ref[h] = code2.astype(BF16)
        b_ref[h] = jnp.exp(s2 - t2[0:1]).astype(BF16)


def _peer_select(q, keys_bf, tb):
    t = q.shape[0]
    big = pl.BlockSpec((PEER_HEADS, N_KEYS, tb), lambda i: (0, 0, i))
    big_shape = jax.ShapeDtypeStruct((PEER_HEADS, N_KEYS, t), F32)
    return pl.pallas_call(
        _peer_select_kernel,
        grid=(t // tb,),
        in_specs=[pl.BlockSpec((tb, PEER_HEADS * PEER_QDIM), lambda i: (i, 0)),
                  _resident((PEER_HEADS, 2, N_KEYS, PEER_HALF))],
        out_specs=[big, big, big, big],
        out_shape=[big_shape, big_shape, jax.ShapeDtypeStruct((PEER_HEADS, N_KEYS, t), BF16),
                   jax.ShapeDtypeStruct((PEER_HEADS, N_KEYS, t), BF16)],
        compiler_params=_params("parallel"),
        name="peer_select",
    )(q, keys_bf)


EXPERT_CHUNK = 1024
KEY_ROWS = EXPERT_CHUNK // N_KEYS


def _gelu(x):
    return 0.5 * x * (1.0 + lax.erf(x * (2.0 ** -0.5)))


def _peer_experts_kernel(u_ref, xbt_ref, vt_ref, thr_ref, a_ref, code_ref, b_ref,
                         o_ref, acc_ref, ht0_ref, ht1_ref, g_ref, *, n_chunks):
    s = pl.program_id(0)
    e = lax.rem(jnp.maximum(s - 1, 0), n_chunks)

    @pl.when(s == 0)
    def _():
        ht1_ref[...] = jnp.zeros_like(ht1_ref)

    @pl.when(e == 0)
    def _():
        acc_ref[...] = jnp.zeros_like(acc_ref)

    tb = xbt_ref.shape[1]

    def spread(row):
        return jnp.broadcast_to(row.astype(BF16), (N_KEYS, tb))

    def gate(ht_in, r):
        w = None
        for h in range(PEER_HEADS):
            term = jnp.where(code_ref[h] >= spread(thr_ref[h, r:r + 1, :]),
                             b_ref[h] * spread(a_ref[h, r:r + 1, :]), jnp.zeros((), BF16))
            w = term if w is None else w + term
        rows = slice(r * N_KEYS, (r + 1) * N_KEYS)
        g_ref[rows, :] = _gelu(ht_in[rows, :]).astype(BF16) * w

    def step(ht_in, ht_out):
        ht_out[...] = jnp.dot(u_ref[...], xbt_ref[...], preferred_element_type=F32)
        for r in range(KEY_ROWS):
            gate(ht_in, r)
        acc_ref[...] += jnp.dot(vt_ref[...], g_ref[...], preferred_element_type=F32)

    @pl.when(lax.rem(s, 2) == 0)
    def _():
        step(ht1_ref, ht0_ref)

    @pl.when(lax.rem(s, 2) == 1)
    def _():
        step(ht0_ref, ht1_ref)

    @pl.when(jnp.logical_and(e == n_chunks - 1, s > 0))
    def _():
        o_ref[...] = acc_ref[...].T


def _transpose_cast_kernel(v_ref, o_ref):
    o_ref[...] = v_ref[...].T.astype(BF16)


def _transposed_bf16(v):
    e, d = v.shape
    return pl.pallas_call(
        _transpose_cast_kernel,
        grid=(e // EXPERT_CHUNK,),
        in_specs=[pl.BlockSpec((EXPERT_CHUNK, d), lambda i: (i, 0))],
        out_specs=pl.BlockSpec((None, d, EXPERT_CHUNK), lambda i: (i, 0, 0)),
        out_shape=jax.ShapeDtypeStruct((e // EXPERT_CHUNK, d, EXPERT_CHUNK), BF16),
        compiler_params=_params("parallel"),
        name="expert_v_layout",
    )(v)


def _peer_experts(xbt, u_bf, vt_bf, thr, a, s2, b, tb):
    t = xbt.shape[1]
    n_chunks = N_EXPERTS // EXPERT_CHUNK
    n = (t // tb) * n_chunks
    lag = lambda s, d: jnp.clip(s - d, 0, n - 1)
    chunk = lambda s, d: lax.rem(lag(s, d), n_chunks)
    block = lambda s, d: lag(s, d) // n_chunks
    rows = pl.BlockSpec((PEER_HEADS, KEY_ROWS, tb), lambda s: (0, chunk(s, 1), block(s, 1)))
    full = pl.BlockSpec((PEER_HEADS, N_KEYS, tb), lambda s: (0, 0, block(s, 1)))
    return pl.pallas_call(
        functools.partial(_peer_experts_kernel, n_chunks=n_chunks),
        grid=(n + 1,),
        in_specs=[pl.BlockSpec((EXPERT_CHUNK, D_MODEL), lambda s: (chunk(s, 0), 0)),
                  pl.BlockSpec((D_MODEL, tb), lambda s: (0, block(s, 0))),
                  pl.BlockSpec((None, D_MODEL, EXPERT_CHUNK), lambda s: (chunk(s, 1), 0, 0)),
                  rows, rows, full, full],
        out_specs=pl.BlockSpec((tb, D_MODEL), lambda s: (block(s, 1), 0)),
        out_shape=jax.ShapeDtypeStruct((t, D_MODEL), F32),
        scratch_shapes=[pltpu.VMEM((D_MODEL, tb), F32), pltpu.VMEM((EXPERT_CHUNK, tb), F32),
                        pltpu.VMEM((EXPERT_CHUNK, tb), F32), pltpu.VMEM((EXPERT_CHUNK, tb), BF16)],
        compiler_params=_params("arbitrary"),
        name="peer_experts",
    )(u_bf, xbt, vt_bf, thr, a, s2, b)


def _final_kernel(h_ref, y_ref, p_ref, gp_ref, wg_ref, wp_ref, gf_ref, o_ref):
    h = h_ref[...] + y_ref[...]
    ple = jnp.dot(p_ref[...].astype(BF16), wp_ref[...], preferred_element_type=F32)
    gate = jax.nn.sigmoid(jnp.dot(_rms(h, gp_ref[...]).astype(BF16), wg_ref[...],
                                  preferred_element_type=F32))
    o_ref[...] = _rms(h + ple * gate, gf_ref[...])


def _final(h, y, p, g_ple, wg_bf, wp_bf, g_final, tm):
    t = h.shape[0]
    row = lambda w: pl.BlockSpec((tm, w), lambda i: (i, 0))
    return pl.pallas_call(
        _final_kernel,
        grid=(t // tm,),
        in_specs=[row(D_MODEL), row(D_MODEL), row(PLE_DIM), _resident((1, D_MODEL)),
                  _resident((D_MODEL, D_MODEL)), _resident((PLE_DIM, D_MODEL)), _resident((1, D_MODEL))],
        out_specs=row(D_MODEL),
        out_shape=jax.ShapeDtypeStruct((t, D_MODEL), F32),
        compiler_params=_params("parallel"),
        name="final",
    )(h, y, p, g_ple.reshape(1, D_MODEL), wg_bf, wp_bf, g_final.reshape(1, D_MODEL))


def _tile(t, pref):
    return pref if t % pref == 0 else t


def _channel_mix(h1, p, w, tm, tb):
    q, xbt = _peer_query(h1, w["g_ffn"], w["wq"], tm)
    thr, a, s2, b = _peer_select(q, w["keys"], tb)
    y = _peer_experts(xbt, w["u"], w["vt"], thr, a, s2, b, tb)
    return _final(h1, y, p, w["g_ple"], w["wg"], w["wp"], w["g_final"], tm)


def kernel(x_prompt, x_sample, cache_k, cache_v, state_conv, p_prompt, p_sample, g_mix, w_in, conv_w,
           attn_sinks, g_attn_out, g_conv_out, w_out, g_ffn, w_peer_q, peer_sub_keys, expert_u,
           expert_v, g_ple, w_ple_gate, w_ple, g_final):
    seq = x_prompt.shape[1]
    nb = x_sample.shape[0]
    w = {
        "g_ffn": g_ffn[0], "wq": w_peer_q[0].astype(BF16), "keys": peer_sub_keys[0].astype(BF16),
        "u": expert_u[0].astype(BF16), "vt": _transposed_bf16(expert_v[0]),
        "g_ple": g_ple[0], "wg": w_ple_gate[0].astype(BF16), "wp": w_ple[0].astype(BF16),
        "g_final": g_final,
    }
    win_bf = w_in[0].astype(BF16)
    wout_bf = w_out[0].astype(BF16)
    sinks = attn_sinks[0]

    xp = x_prompt[0]
    tm = _tile(seq, 512)
    tb = _tile(seq, 512)
    q, k, v, gb, u = _in_proj(xp, g_mix[0], win_bf, jnp.arange(seq, dtype=jnp.int32), tm)
    h1 = _mixer_prompt(q, k, v, gb, u, xp, wout_bf, conv_w[0], sinks, g_attn_out[0], g_conv_out[0])
    y_prompt = _channel_mix(h1, p_prompt[0, 0], w, tm, tb)[None]
    new_k_prompt = k[seq - WINDOW:].reshape(1, 1, WINDOW, N_KV_HEADS, HEAD_DIM)
    new_v_prompt = v[seq - WINDOW:].reshape(1, 1, WINDOW, N_KV_HEADS, HEAD_DIM)
    new_conv_prompt = u[seq - (CONV_K - 1):].reshape(1, 1, CONV_K - 1, CONV_WIDTH)

    xs = x_sample[:, 0]
    pos_s = jnp.full((nb,), PAST_LEN, jnp.int32)
    qs, ks, vs, gbs, us = _in_proj(xs, g_mix[0], win_bf, pos_s, nb)
    attn_s, nk, nv = _attn_sample(qs, ks, vs, cache_k[0].reshape(nb, WINDOW, KV_WIDTH),
                                  cache_v[0].reshape(nb, WINDOW, KV_WIDTH), sinks)
    h1s, ns = _mix_sample(attn_s, gbs, us, state_conv[0].reshape(nb, (CONV_K - 1) * CONV_WIDTH), xs,
                          wout_bf, conv_w[0], g_attn_out[0], g_conv_out[0])
    y_sample = _channel_mix(h1s, p_sample[0, :, 0], w, nb, nb)[:, None]
    new_k_sample = nk.reshape(1, nb, WINDOW, N_KV_HEADS, HEAD_DIM)
    new_v_sample = nv.reshape(1, nb, WINDOW, N_KV_HEADS, HEAD_DIM)
    new_conv_sample = ns.reshape(1, nb, CONV_K - 1, CONV_WIDTH)

    return (y_prompt, y_sample, new_k_prompt, new_v_prompt, new_conv_prompt,
            new_k_sample, new_v_sample, new_conv_sample)
```

```python
import functools

import jax
import jax.numpy as jnp
from jax import lax
from jax.experimental import pallas as pl
from jax.experimental.pallas import tpu as pltpu

F32 = jnp.float32
BF16 = jnp.bfloat16

D_MODEL = 2048
HEAD_DIM = 64
N_HEADS = 16
N_KV_HEADS = 4
GROUP = N_HEADS // N_KV_HEADS
ATTN_WIDTH = N_HEADS * HEAD_DIM
KV_WIDTH = N_KV_HEADS * HEAD_DIM
CONV_WIDTH = D_MODEL - ATTN_WIDTH
CONV_K = 3
WINDOW = 128
ATTN_SCALE = HEAD_DIM ** -0.5
ROPE_THETA = 500000.0
ROPE_DIM = HEAD_DIM // 4
ROPE_HALF = ROPE_DIM // 2
IN_COLS = ATTN_WIDTH + 2 * KV_WIDTH + 3 * CONV_WIDTH
PAST_LEN = 16384
PEER_HEADS = 8
N_KEYS = 128
N_EXPERTS = N_KEYS * N_KEYS
PEER_TOPK = 16
PEER_QDIM = 256
PEER_HALF = PEER_QDIM // 2
PLE_DIM = 256
EPS = 1e-6
NEG = -1e30
NEG_INF = float("-inf")

LANES = 128
SUBLANES = 8
VMEM_LIMIT = 56 * 1024 * 1024


def _params(*sem, flags=None):
    return pltpu.CompilerParams(dimension_semantics=sem, vmem_limit_bytes=VMEM_LIMIT, flags=flags)


def _resident(shape):
    zeros = (0,) * len(shape)
    return pl.BlockSpec(shape, lambda *_: zeros, pipeline_mode=pl.Buffered(1))


def _rms(x, g):
    ms = jnp.mean(x * x, axis=-1, keepdims=True)
    return (x * lax.rsqrt(ms + EPS)) * g


def _in_proj_kernel(x_ref, g_ref, w_ref, c_ref, sa_ref, sb_ref,
                    q_ref, k_ref, v_ref, gb_ref, u_ref):
    a = _rms(x_ref[...], g_ref[...])
    z = jnp.dot(a.astype(BF16), w_ref[...], preferred_element_type=F32)
    o_k = ATTN_WIDTH
    o_v = o_k + KV_WIDTH
    o_gb = o_v + KV_WIDTH
    o_gc = o_gb + CONV_WIDTH
    o_xc = o_gc + CONV_WIDTH

    def rope(t):
        width = t.shape[1]
        reps = width // LANES
        c = jnp.concatenate([c_ref[...]] * reps, axis=1)
        sa = jnp.concatenate([sa_ref[...]] * reps, axis=1)
        sb = jnp.concatenate([sb_ref[...]] * reps, axis=1)
        up = pltpu.roll(t, width - ROPE_HALF, axis=1)
        dn = pltpu.roll(t, ROPE_HALF, axis=1)
        return t * c + up * sa + dn * sb

    q_ref[...] = rope(z[:, :o_k])
    k_ref[...] = rope(z[:, o_k:o_v])
    v_ref[...] = z[:, o_v:o_gb]
    gb_ref[...] = z[:, o_gb:o_gc]
    u_ref[...] = z[:, o_gc:o_xc] * z[:, o_xc:]


def _rope_tables(pos):
    inv = ROPE_THETA ** (-jnp.arange(ROPE_HALF, dtype=F32) / ROPE_HALF)
    lane = jnp.arange(LANES) % HEAD_DIM
    inv_lane = jnp.where(lane < ROPE_DIM, inv[lane % ROPE_HALF], 0.0)
    ang = pos.astype(F32)[:, None] * inv_lane[None, :]
    cos, sin = jnp.cos(ang), jnp.sin(ang)
    sa = jnp.where(lane < ROPE_HALF, -sin, 0.0)
    sb = jnp.where(lane >= ROPE_HALF, sin, 0.0)
    return cos, sa, sb


def _in_proj(x, g, w_bf, pos, tm):
    t = x.shape[0]
    c, sa, sb = _rope_tables(pos)
    row = lambda w: pl.BlockSpec((tm, w), lambda i: (i, 0))
    return pl.pallas_call(
        _in_proj_kernel,
        grid=(t // tm,),
        in_specs=[row(D_MODEL), _resident((1, D_MODEL)), _resident((D_MODEL, IN_COLS)),
                  row(LANES), row(LANES), row(LANES)],
        out_specs=[row(ATTN_WIDTH), row(KV_WIDTH), row(KV_WIDTH), row(CONV_WIDTH), row(CONV_WIDTH)],
        out_shape=[jax.ShapeDtypeStruct((t, ATTN_WIDTH), F32),
                   jax.ShapeDtypeStruct((t, KV_WIDTH), F32),
                   jax.ShapeDtypeStruct((t, KV_WIDTH), F32),
                   jax.ShapeDtypeStruct((t, CONV_WIDTH), F32),
                   jax.ShapeDtypeStruct((t, CONV_WIDTH), F32)],
        compiler_params=_params("parallel"),
        name="in_proj",
    )(x, g.reshape(1, D_MODEL), w_bf, c, sa, sb)


def _mix_tail(attn, conv_out, x, ga, gc, wout):
    merged = jnp.concatenate([_rms(attn, ga), _rms(conv_out, gc)], axis=1)
    return x + jnp.dot(merged.astype(BF16), wout, preferred_element_type=F32)


MIX_BLOCKS = 2


def _mixer_prompt_kernel(sinks_ref, q_ref, kp_ref, kc_ref, vp_ref, vc_ref, gb_ref, up_ref, uc_ref,
                         x_ref, wout_ref, convw_ref, ga_ref, gc_ref, h_ref):
    n = pl.program_id(0)
    rows = MIX_BLOCKS * WINDOW
    kall = jnp.concatenate([kp_ref[...], kc_ref[...]], axis=0).astype(BF16)
    vall = jnp.concatenate([vp_ref[...], vc_ref[...]], axis=0).astype(BF16)
    qi = lax.broadcasted_iota(jnp.int32, (WINDOW, 2 * WINDOW), 0) + WINDOW
    kj = lax.broadcasted_iota(jnp.int32, (WINDOW, 2 * WINDOW), 1)
    diff = qi - kj
    band = jnp.where(diff >= 0, jnp.where(diff < WINDOW, 1, 0), 0)
    first = jnp.where(kj + n * rows >= WINDOW, band, 0) > 0
    band = band > 0
    blocks = []
    for j in range(MIX_BLOCKS):
        mask = first if j == 0 else band
        q = q_ref[j * WINDOW:(j + 1) * WINDOW, :]
        kband = kall[j * WINDOW:(j + 2) * WINDOW]
        vband = vall[j * WINDOW:(j + 2) * WINDOW]
        outs = []
        for h in range(N_HEADS):
            g = h // GROUP
            qh = q[:, h * HEAD_DIM:(h + 1) * HEAD_DIM].astype(BF16)
            kg = kband[:, g * HEAD_DIM:(g + 1) * HEAD_DIM]
            vg = vband[:, g * HEAD_DIM:(g + 1) * HEAD_DIM]
            s = lax.dot_general(qh, kg, (((1,), (1,)), ((), ())), preferred_element_type=F32) * ATTN_SCALE
            s = jnp.where(mask, s, NEG)
            sk = sinks_ref[h]
            m = jnp.maximum(jnp.max(s, axis=-1, keepdims=True), sk)
            e = jnp.where(mask, jnp.exp(s - m), 0.0)
            den = jnp.sum(e, axis=-1, keepdims=True) + jnp.exp(sk - m)
            p = e / den
            outs.append(jnp.dot(p.astype(BF16), vg, preferred_element_type=F32))
        blocks.append(jnp.concatenate(outs, axis=1))
    attn = jnp.concatenate(blocks, axis=0)

    u = uc_ref[...]
    prev = jnp.where(n > 0, up_ref[...], 0.0)
    ext = jnp.concatenate([prev, u], axis=0)
    cw = convw_ref[...]
    y = (ext[SUBLANES - 2:SUBLANES - 2 + rows] * cw[0:1]
         + ext[SUBLANES - 1:SUBLANES - 1 + rows] * cw[1:2]
         + u * cw[2:3])
    conv_out = gb_ref[...] * y
    h_ref[...] = _mix_tail(attn, conv_out, x_ref[...], ga_ref[...], gc_ref[...], wout_ref[...])


def _mixer_prompt(q, k, v, gb, u, x, wout_bf, conv_w, sinks, ga, gc):
    t = q.shape[0]
    rows = MIX_BLOCKS * WINDOW
    cur = lambda w: pl.BlockSpec((rows, w), lambda n: (n, 0))
    prv = lambda w: pl.BlockSpec((WINDOW, w), lambda n: (jnp.maximum(n * MIX_BLOCKS - 1, 0), 0))
    tail = pl.BlockSpec((SUBLANES, CONV_WIDTH), lambda n: (jnp.maximum(n * (rows // SUBLANES) - 1, 0), 0))
    return pl.pallas_call(
        _mixer_prompt_kernel,
        grid=(t // rows,),
        in_specs=[pl.BlockSpec(memory_space=pltpu.SMEM),
                  cur(ATTN_WIDTH), prv(KV_WIDTH), cur(KV_WIDTH), prv(KV_WIDTH), cur(KV_WIDTH),
                  cur(CONV_WIDTH), tail, cur(CONV_WIDTH), cur(D_MODEL),
                  _resident((D_MODEL, D_MODEL)), _resident((SUBLANES, CONV_WIDTH)),
                  _resident((1, ATTN_WIDTH)), _resident((1, CONV_WIDTH))],
        out_specs=cur(D_MODEL),
        out_shape=jax.ShapeDtypeStruct((t, D_MODEL), F32),
        compiler_params=_params("parallel"),
        name="mixer_prompt",
    )(sinks, q, k, k, v, v, gb, u, u, x, wout_bf,
      jnp.pad(conv_w, ((0, SUBLANES - CONV_K), (0, 0))), ga.reshape(1, -1), gc.reshape(1, -1))


SAMPLE_ROWS = 8


def _attn_sample_kernel(q_ref, kn_ref, vn_ref, ck_ref, cv_ref, sk_ref, o_ref, nk_ref, nv_ref):
    hl = lax.broadcasted_iota(jnp.int32, (N_HEADS, KV_WIDTH), 1) // HEAD_DIM
    hg = lax.broadcasted_iota(jnp.int32, (N_HEADS, KV_WIDTH), 0) // GROUP
    own = hl == hg
    last = lax.broadcasted_iota(jnp.int32, (WINDOW, KV_WIDTH), 0) == WINDOW - 1
    sk = sk_ref[...][:, 0:1]
    kn = kn_ref[...]
    vn = vn_ref[...]
    for b in range(SAMPLE_ROWS):
        nk = jnp.where(last, kn[b:b + 1], pltpu.roll(ck_ref[b], WINDOW - 1, axis=0))
        nv = jnp.where(last, vn[b:b + 1], pltpu.roll(cv_ref[b], WINDOW - 1, axis=0))
        nk_ref[b] = nk
        nv_ref[b] = nv
        qb = q_ref[b]
        qw = jnp.where(own, jnp.concatenate([qb] * N_KV_HEADS, axis=1), 0.0)
        s = lax.dot_general(qw.astype(BF16), nk.astype(BF16), (((1,), (1,)), ((), ())),
                            preferred_element_type=F32) * ATTN_SCALE
        m = jnp.maximum(jnp.max(s, axis=-1, keepdims=True), sk)
        e = jnp.exp(s - m)
        p = e / (jnp.sum(e, axis=-1, keepdims=True) + jnp.exp(sk - m))
        pv = jnp.where(own, jnp.dot(p.astype(BF16), nv.astype(BF16), preferred_element_type=F32), 0.0)
        o = pv[:, 0:HEAD_DIM]
        for g in range(1, N_KV_HEADS):
            o = o + pv[:, g * HEAD_DIM:(g + 1) * HEAD_DIM]
        o_ref[b] = o


def _attn_sample(q, kn, vn, ck, cv, sinks):
    b = q.shape[0]
    q3 = q.reshape(b, N_HEADS, HEAD_DIM)
    sk = jnp.broadcast_to(sinks.reshape(N_HEADS, 1), (N_HEADS, LANES))
    r = SAMPLE_ROWS
    o3, nk, nv = pl.pallas_call(
        _attn_sample_kernel,
        grid=(b // r,),
        in_specs=[pl.BlockSpec((r, N_HEADS, HEAD_DIM), lambda i: (i, 0, 0)),
                  pl.BlockSpec((r, KV_WIDTH), lambda i: (i, 0)),
                  pl.BlockSpec((r, KV_WIDTH), lambda i: (i, 0)),
                  pl.BlockSpec((r, WINDOW, KV_WIDTH), lambda i: (i, 0, 0)),
                  pl.BlockSpec((r, WINDOW, KV_WIDTH), lambda i: (i, 0, 0)),
                  _resident((N_HEADS, LANES))],
        out_specs=[pl.BlockSpec((r, N_HEADS, HEAD_DIM), lambda i: (i, 0, 0)),
                   pl.BlockSpec((r, WINDOW, KV_WIDTH), lambda i: (i, 0, 0)),
                   pl.BlockSpec((r, WINDOW, KV_WIDTH), lambda i: (i, 0, 0))],
        out_shape=[jax.ShapeDtypeStruct((b, N_HEADS, HEAD_DIM), F32),
                   jax.ShapeDtypeStruct((b, WINDOW, KV_WIDTH), F32),
                   jax.ShapeDtypeStruct((b, WINDOW, KV_WIDTH), F32)],
        compiler_params=_params("parallel"),
        name="attn_sample",
    )(q3, kn, vn, ck, cv, sk)
    return o3.reshape(b, ATTN_WIDTH), nk, nv


def _mix_sample_kernel(attn_ref, gb_ref, u_ref, st_ref, x_ref, wout_ref, convw_ref, ga_ref, gc_ref,
                       h_ref, ns_ref):
    u = u_ref[...]
    st = st_ref[...]
    p0 = st[:, :CONV_WIDTH]
    p1 = st[:, CONV_WIDTH:]
    cw = convw_ref[...]
    y = p0 * cw[0:1] + p1 * cw[1:2] + u * cw[2:3]
    ns_ref[...] = jnp.concatenate([p1, u], axis=1)
    h_ref[...] = _mix_tail(attn_ref[...], gb_ref[...] * y, x_ref[...], ga_ref[...], gc_ref[...],
                           wout_ref[...])


def _mix_sample(attn, gb, u, state2, x, wout_bf, conv_w, ga, gc):
    b = attn.shape[0]
    full = lambda w: pl.BlockSpec((b, w), lambda i: (0, 0))
    return pl.pallas_call(
        _mix_sample_kernel,
        grid=(1,),
        in_specs=[full(ATTN_WIDTH), full(CONV_WIDTH), full(CONV_WIDTH), full(2 * CONV_WIDTH),
                  full(D_MODEL), _resident((D_MODEL, D_MODEL)), _resident((SUBLANES, CONV_WIDTH)),
                  _resident((1, ATTN_WIDTH)), _resident((1, CONV_WIDTH))],
        out_specs=[full(D_MODEL), full(2 * CONV_WIDTH)],
        out_shape=[jax.ShapeDtypeStruct((b, D_MODEL), F32),
                   jax.ShapeDtypeStruct((b, 2 * CONV_WIDTH), F32)],
        compiler_params=_params("arbitrary"),
        name="mix_sample",
    )(attn, gb, u, state2, x, wout_bf,
      jnp.pad(conv_w, ((0, SUBLANES - CONV_K), (0, 0))), ga.reshape(1, -1), gc.reshape(1, -1))


def _sort16_pairs():
    def merge(lo, hi, r):
        step = r * 2
        if step < hi - lo:
            yield from merge(lo, hi, step)
            yield from merge(lo + r, hi, step)
            yield from [(i, i + r) for i in range(lo + r, hi - r, step)]
        else:
            yield (lo, lo + r)

    def sort(lo, hi):
        if hi - lo >= 1:
            mid = lo + (hi - lo) // 2
            yield from sort(lo, mid)
            yield from sort(mid + 1, hi)
            yield from merge(lo, hi, 1)

    return list(sort(0, PEER_TOPK - 1))


def _top_sorted(s, with_code=False):
    n_v = N_KEYS // SUBLANES
    assert n_v == PEER_TOPK
    tops, codes = [], []
    for c in range(s.shape[1] // LANES):
        cols = slice(c * LANES, (c + 1) * LANES)
        x = [s[i * SUBLANES:(i + 1) * SUBLANES, cols] for i in range(n_v)]
        v = list(x)

        def exchange(i, j):
            v[i], v[j] = jnp.maximum(v[i], v[j]), jnp.minimum(v[i], v[j])

        for i, j in _sort16_pairs():
            exchange(i, j)
        shift = SUBLANES // 2
        while shift >= 1:
            rolled = [pltpu.roll(t, shift, axis=0) for t in v]
            v = [jnp.maximum(v[i], rolled[n_v - 1 - i]) for i in range(n_v)]
            d = n_v // 2
            while d >= 1:
                for i in range(n_v):
                    if not i & d:
                        exchange(i, i + d)
                d //= 2
            shift //= 2
        tops.append(jnp.concatenate([t[0:1] for t in v], axis=0))
        if with_code:
            parts = []
            for xi in x:
                cnt = jnp.where(xi >= v[0], 1.0, 0.0)
                for t in v[1:]:
                    cnt = cnt + jnp.where(xi >= t, 1.0, 0.0)
                parts.append(cnt)
            codes.append(jnp.concatenate(parts, axis=0))
    top = jnp.concatenate(tops, axis=1)
    return top, (jnp.concatenate(codes, axis=1) if with_code else None)


def _peer_route_kernel(h_ref, g_ref, w_ref, keys_ref, xbt_ref, thr_ref, a_ref, code_ref, b_ref):
    x = _rms(h_ref[...], g_ref[...])
    xbt_ref[...] = x.T.astype(BF16)
    q = jnp.dot(x.astype(BF16), w_ref[...], preferred_element_type=F32)
    _select_heads(q.astype(BF16), keys_ref, thr_ref, a_ref, code_ref, b_ref)


def _select_heads(q, keys_ref, thr_ref, a_ref, code_ref, b_ref):
    tb = q.shape[0]
    for h in range(PEER_HEADS):
        sc = []
        for c in range(2):
            col = (2 * h + c) * PEER_HALF
            sc.append(lax.dot_general(keys_ref[h, c], q[:, col:col + PEER_HALF],
                                      (((1,), (1,)), ((), ())), preferred_element_type=F32))
        s1, s2 = sc
        t1, _ = _top_sorted(s1)
        t2, code2 = _top_sorted(s2, with_code=True)
        r16 = lax.broadcasted_iota(jnp.int32, (16, tb), 0)
        r8 = lax.broadcasted_iota(jnp.int32, (8, tb), 0)
        pieces = [t1[0:1] + t2]
        for r in range(1, 5):
            pieces.append(t1[r:r + 1] + t2[0:8])
        pieces.append(jnp.where(r16 >= 5, t1 + t2[0:1], NEG_INF))
        pieces.append(jnp.where(r8 >= 5, t1[0:8] + t2[1:2], NEG_INF))
        cand = jnp.concatenate(pieces, axis=0)
        work = cand
        for _ in range(PEER_TOPK):
            tau = jnp.max(work, axis=0, keepdims=True)
            work = jnp.where(work >= tau, NEG_INF, work)
        top = t1[0:1] + t2[0:1]
        z = jnp.sum(jnp.where(cand >= tau, jnp.exp(cand - top), 0.0), axis=0, keepdims=True)
        thr_rank = jnp.full((PEER_TOPK, tb), float(PEER_TOPK + 1), F32)
        for c in range(PEER_TOPK):
            thr_rank = thr_rank - jnp.where(t1 + t2[c:c + 1] >= tau, 1.0, 0.0)
        thr = jnp.full((N_KEYS, tb), float(PEER_TOPK + 1), F32)
        for r in range(PEER_TOPK):
            thr = jnp.where(s1 == t1[r:r + 1], thr_rank[r:r + 1], thr)
        thr_ref[h] = thr
        a_ref[h] = jnp.exp(s1 - t1[0:1]) / z
        code_ref[h] = code2.astype(BF16)
        b_ref[h] = jnp.exp(s2 - t2[0:1]).astype(BF16)


def _peer_route(h, g, wq_bf, keys_bf, tb):
    t = h.shape[0]
    big = pl.BlockSpec((PEER_HEADS, N_KEYS, tb), lambda i: (0, 0, i))
    big_shape = jax.ShapeDtypeStruct((PEER_HEADS, N_KEYS, t), F32)
    return pl.pallas_call(
        _peer_route_kernel,
        grid=(t // tb,),
        in_specs=[pl.BlockSpec((tb, D_MODEL), lambda i: (i, 0)), _resident((1, D_MODEL)),
                  _resident((D_MODEL, PEER_HEADS * PEER_QDIM)),
                  _resident((PEER_HEADS, 2, N_KEYS, PEER_HALF))],
        out_specs=[pl.BlockSpec((D_MODEL, tb), lambda i: (0, i)), big, big, big, big],
        out_shape=[jax.ShapeDtypeStruct((D_MODEL, t), BF16), big_shape, big_shape,
                   jax.ShapeDtypeStruct((PEER_HEADS, N_KEYS, t), BF16),
                   jax.ShapeDtypeStruct((PEER_HEADS, N_KEYS, t), BF16)],
        compiler_params=_params("parallel"),
        name="peer_route",
    )(h, g.reshape(1, D_MODEL), wq_bf, keys_bf)


EXPERT_CHUNK = 1024
KEY_ROWS = EXPERT_CHUNK // N_KEYS


def _gelu(x):
    return 0.5 * x * (1.0 + lax.erf(x * (2.0 ** -0.5)))


def _peer_experts_kernel(u_ref, xbt_ref, vt_ref, thr_ref, a_ref, code_ref, b_ref,
                         o_ref, acc_ref, ht0_ref, ht1_ref, g_ref, *, n_chunks):
    s = pl.program_id(0)
    e = lax.rem(jnp.maximum(s - 1, 0), n_chunks)

    @pl.when(s == 0)
    def _():
        ht1_ref[...] = jnp.zeros_like(ht1_ref)

    @pl.when(e == 0)
    def _():
        acc_ref[...] = jnp.zeros_like(acc_ref)

    tb = xbt_ref.shape[1]

    def spread(row):
        return jnp.broadcast_to(row.astype(BF16), (N_KEYS, tb))

    def gate(ht_in, r):
        w = None
        for h in range(PEER_HEADS):
            term = jnp.where(code_ref[h] >= spread(thr_ref[h, r:r + 1, :]),
                             b_ref[h] * spread(a_ref[h, r:r + 1, :]), jnp.zeros((), BF16))
            w = term if w is None else w + term
        rows = slice(r * N_KEYS, (r + 1) * N_KEYS)
        g_ref[rows, :] = _gelu(ht_in[rows, :]).astype(BF16) * w

    def step(ht_in, ht_out):
        ht_out[...] = jnp.dot(u_ref[...], xbt_ref[...], preferred_element_type=F32)
        for r in range(KEY_ROWS):
            gate(ht_in, r)
        acc_ref[...] += jnp.dot(vt_ref[...], g_ref[...], preferred_element_type=F32)

    @pl.when(lax.rem(s, 2) == 0)
    def _():
        step(ht1_ref, ht0_ref)

    @pl.when(lax.rem(s, 2) == 1)
    def _():
        step(ht0_ref, ht1_ref)

    @pl.when(jnp.logical_and(e == n_chunks - 1, s > 0))
    def _():
        o_ref[...] = acc_ref[...].T


def _transpose_cast_kernel(v_ref, o_ref):
    o_ref[...] = v_ref[...].T.astype(BF16)


def _transposed_bf16(v):
    e, d = v.shape
    return pl.pallas_call(
        _transpose_cast_kernel,
        grid=(e // EXPERT_CHUNK,),
        in_specs=[pl.BlockSpec((EXPERT_CHUNK, d), lambda i: (i, 0))],
        out_specs=pl.BlockSpec((None, d, EXPERT_CHUNK), lambda i: (i, 0, 0)),
        out_shape=jax.ShapeDtypeStruct((e // EXPERT_CHUNK, d, EXPERT_CHUNK), BF16),
        compiler_params=_params("parallel"),
        name="expert_v_layout",
    )(v)


def _peer_experts(xbt, u_bf, vt_bf, thr, a, s2, b, tb):
    t = xbt.shape[1]
    n_chunks = N_EXPERTS // EXPERT_CHUNK
    n = (t // tb) * n_chunks
    lag = lambda s, d: jnp.clip(s - d, 0, n - 1)
    chunk = lambda s, d: lax.rem(lag(s, d), n_chunks)
    block = lambda s, d: lag(s, d) // n_chunks
    rows = pl.BlockSpec((PEER_HEADS, KEY_ROWS, tb), lambda s: (0, chunk(s, 1), block(s, 1)))
    full = pl.BlockSpec((PEER_HEADS, N_KEYS, tb), lambda s: (0, 0, block(s, 1)))
    return pl.pallas_call(
        functools.partial(_peer_experts_kernel, n_chunks=n_chunks),
        grid=(n + 1,),
        in_specs=[pl.BlockSpec((EXPERT_CHUNK, D_MODEL), lambda s: (chunk(s, 0), 0)),
                  pl.BlockSpec((D_MODEL, tb), lambda s: (0, block(s, 0))),
                  pl.BlockSpec((None, D_MODEL, EXPERT_CHUNK), lambda s: (chunk(s, 1), 0, 0)),
                  rows, rows, full, full],
        out_specs=pl.BlockSpec((tb, D_MODEL), lambda s: (block(s, 1), 0)),
        out_shape=jax.ShapeDtypeStruct((t, D_MODEL), F32),
        scratch_shapes=[pltpu.VMEM((D_MODEL, tb), F32), pltpu.VMEM((EXPERT_CHUNK, tb), F32),
                        pltpu.VMEM((EXPERT_CHUNK, tb), F32), pltpu.VMEM((EXPERT_CHUNK, tb), BF16)],
        compiler_params=_params("arbitrary"),
        name="peer_experts",
    )(u_bf, xbt, vt_bf, thr, a, s2, b)


def _final_kernel(h_ref, y_ref, p_ref, gp_ref, wg_ref, wp_ref, gf_ref, o_ref):
    h = h_ref[...] + y_ref[...]
    ple = jnp.dot(p_ref[...].astype(BF16), wp_ref[...], preferred_element_type=F32)
    gate = jax.nn.sigmoid(jnp.dot(_rms(h, gp_ref[...]).astype(BF16), wg_ref[...],
                                  preferred_element_type=F32))
    o_ref[...] = _rms(h + ple * gate, gf_ref[...])


def _final(h, y, p, g_ple, wg_bf, wp_bf, g_final, tm):
    t = h.shape[0]
    row = lambda w: pl.BlockSpec((tm, w), lambda i: (i, 0))
    return pl.pallas_call(
        _final_kernel,
        grid=(t // tm,),
        in_specs=[row(D_MODEL), row(D_MODEL), row(PLE_DIM), _resident((1, D_MODEL)),
                  _resident((D_MODEL, D_MODEL)), _resident((PLE_DIM, D_MODEL)), _resident((1, D_MODEL))],
        out_specs=row(D_MODEL),
        out_shape=jax.ShapeDtypeStruct((t, D_MODEL), F32),
        compiler_params=_params("parallel"),
        name="final",
    )(h, y, p, g_ple.reshape(1, D_MODEL), wg_bf, wp_bf, g_final.reshape(1, D_MODEL))


def _tile(t, pref):
    return pref if t % pref == 0 else t


def _channel_mix(h1, p, w, tm, tb):
    xbt, thr, a, s2, b = _peer_route(h1, w["g_ffn"], w["wq"], w["keys"], tb)
    y = _peer_experts(xbt, w["u"], w["vt"], thr, a, s2, b, tb)
    return _final(h1, y, p, w["g_ple"], w["wg"], w["wp"], w["g_final"], tm)


def kernel(x_prompt, x_sample, cache_k, cache_v, state_conv, p_prompt, p_sample, g_mix, w_in, conv_w,
           attn_sinks, g_attn_out, g_conv_out, w_out, g_ffn, w_peer_q, peer_sub_keys, expert_u,
           expert_v, g_ple, w_ple_gate, w_ple, g_final):
    seq = x_prompt.shape[1]
    nb = x_sample.shape[0]
    w = {
        "g_ffn": g_ffn[0], "wq": w_peer_q[0].astype(BF16), "keys": peer_sub_keys[0].astype(BF16),
        "u": expert_u[0].astype(BF16), "vt": _transposed_bf16(expert_v[0]),
        "g_ple": g_ple[0], "wg": w_ple_gate[0].astype(BF16), "wp": w_ple[0].astype(BF16),
        "g_final": g_final,
    }
    win_bf = w_in[0].astype(BF16)
    wout_bf = w_out[0].astype(BF16)
    sinks = attn_sinks[0]

    xp = x_prompt[0]
    tm = _tile(seq, 512)
    tb = _tile(seq, 512)
    q, k, v, gb, u = _in_proj(xp, g_mix[0], win_bf, jnp.arange(seq, dtype=jnp.int32), tm)
    h1 = _mixer_prompt(q, k, v, gb, u, xp, wout_bf, conv_w[0], sinks, g_attn_out[0], g_conv_out[0])
    y_prompt = _channel_mix(h1, p_prompt[0, 0], w, tm, tb)[None]
    new_k_prompt = k[seq - WINDOW:].reshape(1, 1, WINDOW, N_KV_HEADS, HEAD_DIM)
    new_v_prompt = v[seq - WINDOW:].reshape(1, 1, WINDOW, N_KV_HEADS, HEAD_DIM)
    new_conv_prompt = u[seq - (CONV_K - 1):].reshape(1, 1, CONV_K - 1, CONV_WIDTH)

    xs = x_sample[:, 0]
    pos_s = jnp.full((nb,), PAST_LEN, jnp.int32)
    qs, ks, vs, gbs, us = _in_proj(xs, g_mix[0], win_bf, pos_s, nb)
    attn_s, nk, nv = _attn_sample(qs, ks, vs, cache_k[0].reshape(nb, WINDOW, KV_WIDTH),
                                  cache_v[0].reshape(nb, WINDOW, KV_WIDTH), sinks)
    h1s, ns = _mix_sample(attn_s, gbs, us, state_conv[0].reshape(nb, (CONV_K - 1) * CONV_WIDTH), xs,
                          wout_bf, conv_w[0], g_attn_out[0], g_conv_out[0])
    y_sample = _channel_mix(h1s, p_sample[0, :, 0], w, nb, nb)[:, None]
    new_k_sample = nk.reshape(1, nb, WINDOW, N_KV_HEADS, HEAD_DIM)
    new_v_sample = nv.reshape(1, nb, WINDOW, N_KV_HEADS, HEAD_DIM)
    new_conv_sample = ns.reshape(1, nb, CONV_K - 1, CONV_WIDTH)

    return (y_prompt, y_sample, new_k_prompt, new_v_prompt, new_conv_prompt,
            new_k_sample, new_v_sample, new_conv_sample)
```
